```python
import math, functools
import jax, jax.numpy as jnp
from jax import lax
import numpy as np

D_MODEL = 1024
BATCH = 1
SEQ = 16384
DEPTH = 2
DEC_BATCH = 32
DEC_SEQ = 1
PAST_LEN = 16384
PAGE_SIZE = 128

H_A = 4
DH_A = 64
DV_A = 2 * DH_A
H_B = 4
DK_B = 64
DV_B = 128
GLA_RANK = 16
GLA_TAU = 16.0
H_C = 4
DK_C = 64
DV_C = 128
ROPE_BASE = 10000.0
W_A = H_A * DV_A
W_B = H_B * DV_B
W_C = H_C * DV_C
NUM_BUCKETS = 32
MAX_DISTANCE = 128
D_FF = 2816
Q_BLOCK = 128
CHUNK = 64
EPS = 1e-6
SIZES = (H_A * 2 * DH_A, H_A * 2 * DH_A, W_A,
         H_B * DK_B, H_B * DK_B, W_B, W_B, GLA_RANK,
         H_C * DK_C, H_C * DK_C, W_C, W_C,
         3 * D_MODEL)
D_IN = sum(SIZES)

kernel_name = 'hybrid_diffattn_gla_retention_macaron'


def rmsnorm(x, g):
    xf = x.astype(jnp.float32)
    y = xf * lax.rsqrt(jnp.mean(xf * xf, axis=-1, keepdims=True) + EPS)
    return (y * g.astype(jnp.float32)).astype(x.dtype)


def layernorm(x, g):
    xf = x.astype(jnp.float32)
    xc = xf - jnp.mean(xf, axis=-1, keepdims=True)
    y = xc * lax.rsqrt(jnp.mean(xc * xc, axis=-1, keepdims=True) + EPS)
    return (y * g.astype(jnp.float32)).astype(x.dtype)


def swiglu(x, wg, wu, wd):
    return (jax.nn.silu(x @ wg) * (x @ wu)) @ wd


def ffn_half(x, g, wg, wu, wd):
    return x + 0.5 * swiglu(rmsnorm(x, g), wg, wu, wd)


def rotary(x, pos):
    half = x.shape[-1] // 2
    inv = ROPE_BASE ** (-jnp.arange(half, dtype=jnp.float32) / half)
    ang = pos.astype(jnp.float32)[:, None] * inv[None, :]
    cos = jnp.cos(ang)[None, :, None, :]
    sin = jnp.sin(ang)[None, :, None, :]
    xf = x.astype(jnp.float32)
    x1, x2 = xf[..., :half], xf[..., half:]
    return jnp.concatenate([x1 * cos - x2 * sin, x1 * sin + x2 * cos], axis=-1).astype(x.dtype)


def t5_bucket(rel):
    n = jnp.maximum(rel, 0)
    max_exact = NUM_BUCKETS // 2
    nf = jnp.maximum(n, max_exact).astype(jnp.float32)
    large = max_exact + (jnp.log(nf / max_exact) / math.log(MAX_DISTANCE / max_exact)
                         * (NUM_BUCKETS - max_exact)).astype(jnp.int32)
    large = jnp.minimum(large, NUM_BUCKETS - 1)
    return jnp.where(n < max_exact, n, large)


def diff_attention(q, k, v, q_pos, k_pos, lam, rel_bias):
    rel = q_pos[:, None] - k_pos[None, :]
    bias = jnp.transpose(rel_bias.astype(jnp.float32)[t5_bucket(rel)], (2, 0, 1))
    s = jnp.einsum('bqhmd,bkhmd->bmhqk', q, k).astype(jnp.float32) * (DH_A ** -0.5) + bias
    s = jnp.where(rel >= 0, s, -1e30)
    p = jax.nn.softmax(s, axis=-1)
    attn = p[:, 0] - lam * p[:, 1]
    return jnp.einsum('bhqk,bkhv->bqhv', attn.astype(v.dtype), v)


def prompt_attend(q, k, v, lam, rel_bias):
    B, L = q.shape[:2]
    nb = L // Q_BLOCK
    pos = jnp.arange(L)
    qb = q.reshape(B, nb, Q_BLOCK, H_A, 2, DH_A).transpose(1, 0, 2, 3, 4, 5)
    out = lax.map(lambda a: diff_attention(a[0], k, v, a[1], pos, lam, rel_bias),
                  (qb, pos.reshape(nb, Q_BLOCK)))
    return out.transpose(1, 0, 2, 3, 4).reshape(B, L, H_A, DV_A)


def sample_attend(q, k, v, lam, rel_bias, cache_k_l, cache_v_l, page_table):
    DB, LS = q.shape[:2]
    past = page_table.shape[1] * PAGE_SIZE
    k_past = cache_k_l[page_table].reshape(DB, past, H_A, 2, DH_A)
    v_past = cache_v_l[page_table].reshape(DB, past, H_A, DV_A)
    k_all = jnp.concatenate([k_past, k.astype(k_past.dtype)], axis=1)
    v_all = jnp.concatenate([v_past, v.astype(v_past.dtype)], axis=1)
    q_pos = past + jnp.arange(LS)
    k_pos = jnp.arange(past + LS)
    return diff_attention(q, k_all, v_all, q_pos, k_pos, lam, rel_bias)


def decay_linear_attention(q, k, v, log_a, s0, chunk):
    B, L, H, DK = q.shape
    DV = v.shape[-1]
    n = L // chunk
    f32 = jnp.float32

    def to_chunks(t):
        return t.reshape(B, n, chunk, H, t.shape[-1]).transpose(1, 0, 3, 2, 4)

    causal = jnp.tril(jnp.ones((chunk, chunk), dtype=bool))

    def step(S, inp):
        qi, ki, vi, gi = [t.astype(f32) for t in inp]
        G = jnp.cumsum(gi, axis=2)
        G_end = G[:, :, -1:, :]
        q_dec = qi * jnp.exp(G)
        k_inv = ki * jnp.exp(-G)
        scores = jnp.where(causal, jnp.einsum('bhtk,bhsk->bhts', q_dec, k_inv), 0.0)
        o = jnp.einsum('bhts,bhsv->bhtv', scores, vi) + jnp.einsum('bhtk,bhkv->bhtv', q_dec, S)
        S = (jnp.exp(G_end[:, :, 0, :])[..., None] * S
             + jnp.einsum('bhsk,bhsv->bhkv', ki * jnp.exp(G_end - G), vi))
        return S, o

    S, o = lax.scan(step, s0.astype(f32), (to_chunks(q), to_chunks(k), to_chunks(v), to_chunks(log_a)))
    o = o.transpose(1, 0, 3, 2, 4).reshape(B, L, H, DV)
    return o.astype(v.dtype), S.astype(s0.dtype)


def token_mixers(h, pos, attend, s_gla0, s_ret0, lam_init, w_in, w_gla_up, b_gla, lambda_qk,
                 norm_a, norm_b, norm_c, w_a, w_b, w_c, w_out):
    B, L, _ = h.shape
    f32 = jnp.float32
    split_at = np.cumsum(SIZES)[:-1].tolist()
    (aq, ak, av, bq, bk, bv, br, bg, cq, ck, cv, cg, gates) = jnp.split(h @ w_in, split_at, axis=-1)
    chunk = math.gcd(CHUNK, L)

    aq = aq.reshape(B, L, H_A, 2, DH_A)
    ak = ak.reshape(B, L, H_A, 2, DH_A)
    av = av.reshape(B, L, H_A, DV_A)
    lqk = lambda_qk.astype(f32)
    lam = jnp.exp(jnp.sum(lqk[0] * lqk[1])) - jnp.exp(jnp.sum(lqk[2] * lqk[3])) + lam_init
    ao = rmsnorm(attend(aq, ak, av, lam), norm_a) * (1.0 - lam_init)
    ya = ao.reshape(B, L, W_A) @ w_a

    bq = bq.reshape(B, L, H_B, DK_B) * (DK_B ** -0.5)
    bk = bk.reshape(B, L, H_B, DK_B)
    bv = bv.reshape(B, L, H_B, DV_B)
    blog = (jax.nn.log_sigmoid((bg @ w_gla_up + b_gla).astype(f32)) / GLA_TAU).reshape(B, L, H_B, DK_B)
    bo, s_gla = decay_linear_attention(bq, bk, bv, blog, s_gla0, chunk)
    yb = (rmsnorm(bo, norm_b).reshape(B, L, W_B) * jax.nn.silu(br)) @ w_b

    cq = rotary(cq.reshape(B, L, H_C, DK_C), pos)
    ck = rotary(ck.reshape(B, L, H_C, DK_C), pos) * (DK_C ** -0.5)
    cv = cv.reshape(B, L, H_C, DV_C)
    log_gamma = jnp.log(1.0 - 2.0 ** (-5.0 - jnp.arange(H_C, dtype=f32)))
    clog = jnp.broadcast_to(log_gamma[None, None, :, None], (B, L, H_C, DK_C))
    co, s_ret = decay_linear_attention(cq, ck, cv, clog, s_ret0, chunk)
    yc = (layernorm(co, norm_c).reshape(B, L, W_C) * jax.nn.silu(cg)) @ w_c

    g_a, g_b, g_c = jnp.split(jax.nn.sigmoid(gates), 3, axis=-1)
    y = (g_a * ya + g_b * yb + g_c * yc) @ w_out
    return y, ak.reshape(B, L, H_A, 2 * DH_A), av, s_gla, s_ret


def setup_inputs(seed: int = 0) -> dict:
    key = jax.random.key(seed)
    ks = iter(jax.random.split(key, 40))
    f32 = jnp.float32

    def nrm(shape, scale):
        return jax.random.normal(next(ks), shape, f32) * scale

    def gain(shape):
        return 1.0 + nrm(shape, 0.05)

    n_pages = PAST_LEN // PAGE_SIZE
    n_used = DEC_BATCH * n_pages
    n_pool = (n_used * 5 + 3) // 4
    page_table = jax.random.permutation(next(ks), n_pool)[:n_used].reshape(DEC_BATCH, n_pages).astype(jnp.int32)
    return {
        'x_prompt': nrm((BATCH, SEQ, D_MODEL), 1.0),
        'x_sample': nrm((DEC_BATCH, DEC_SEQ, D_MODEL), 1.0),
        'cache_k': nrm((DEPTH, n_pool, PAGE_SIZE, H_A, 2 * DH_A), 1.0),
        'cache_v': nrm((DEPTH, n_pool, PAGE_SIZE, H_A, DV_A), 1.0),
        'state_gla': nrm((DEPTH, DEC_BATCH, H_B, DK_B, DV_B), 0.5),
        'state_ret': nrm((DEPTH, DEC_BATCH, H_C, DK_C, DV_C), 0.5),
        'page_table': page_table,
        'w_in': nrm((DEPTH, D_MODEL, D_IN), D_MODEL ** -0.5),
        'w_gla_up': nrm((DEPTH, GLA_RANK, H_B * DK_B), GLA_RANK ** -0.5),
        'b_gla': nrm((DEPTH, H_B * DK_B), 0.1),
        'lambda_qk': nrm((DEPTH, 4, DH_A), 0.1),
        'rel_bias': nrm((NUM_BUCKETS, H_A), 0.5),
        'norm_a': gain((DEPTH, DV_A)),
        'norm_b': gain((DEPTH, H_B, DV_B)),
        'norm_c': gain((DEPTH, H_C, DV_C)),
        'w_a': nrm((DEPTH, W_A, D_MODEL), W_A ** -0.5),
        'w_b': nrm((DEPTH, W_B, D_MODEL), W_B ** -0.5),
        'w_c': nrm((DEPTH, W_C, D_MODEL), W_C ** -0.5),
        'w_out': nrm((DEPTH, D_MODEL, D_MODEL), D_MODEL ** -0.5),
        'norm_ffn1': gain((DEPTH, D_MODEL)),
        'ffn1_wg': nrm((DEPTH, D_MODEL, D_FF), D_MODEL ** -0.5),
        'ffn1_wu': nrm((DEPTH, D_MODEL, D_FF), D_MODEL ** -0.5),
        'ffn1_wd': nrm((DEPTH, D_FF, D_MODEL), D_FF ** -0.5),
        'norm_mix': gain((DEPTH, D_MODEL)),
        'norm_ffn2': gain((DEPTH, D_MODEL)),
        'ffn2_wg': nrm((DEPTH, D_MODEL, D_FF), D_MODEL ** -0.5),
        'ffn2_wu': nrm((DEPTH, D_MODEL, D_FF), D_MODEL ** -0.5),
        'ffn2_wd': nrm((DEPTH, D_FF, D_MODEL), D_FF ** -0.5),
        'norm_final': gain((D_MODEL,)),
    }


def reference(x_prompt, x_sample, cache_k, cache_v, state_gla, state_ret, page_table,
              w_in, w_gla_up, b_gla, lambda_qk, rel_bias, norm_a, norm_b, norm_c,
              w_a, w_b, w_c, w_out, norm_ffn1, ffn1_wg, ffn1_wu, ffn1_wd,
              norm_mix, norm_ffn2, ffn2_wg, ffn2_wu, ffn2_wd, norm_final):
    B, L = x_prompt.shape[:2]
    LS = x_sample.shape[1]
    past = page_table.shape[1] * PAGE_SIZE
    pos_p = jnp.arange(L)
    pos_s = past + jnp.arange(LS)
    gla0_p = jnp.zeros((B, H_B, DK_B, DV_B), x_prompt.dtype)
    ret0_p = jnp.zeros((B, H_C, DK_C, DV_C), x_prompt.dtype)
    attend_p = functools.partial(prompt_attend, rel_bias=rel_bias)

    xp, xs = x_prompt, x_sample
    kp_l, vp_l, gp_l, rp_l = [], [], [], []
    ks_l, vs_l, gs_l, rs_l = [], [], [], []
    for l in range(DEPTH):
        lam_init = 0.8 - 0.6 * math.exp(-0.3 * l)
        mix_w = (w_in[l], w_gla_up[l], b_gla[l], lambda_qk[l], norm_a[l], norm_b[l], norm_c[l],
                 w_a[l], w_b[l], w_c[l], w_out[l])
        ffn1_w = (norm_ffn1[l], ffn1_wg[l], ffn1_wu[l], ffn1_wd[l])
        ffn2_w = (norm_ffn2[l], ffn2_wg[l], ffn2_wu[l], ffn2_wd[l])

        xp = ffn_half(xp, *ffn1_w)
        yp, kp, vp, gp, rp = token_mixers(rmsnorm(xp, norm_mix[l]), pos_p, attend_p,
                                          gla0_p, ret0_p, lam_init, *mix_w)
        xp = ffn_half(xp + yp, *ffn2_w)

        attend_s = functools.partial(sample_attend, rel_bias=rel_bias, cache_k_l=cache_k[l],
                                     cache_v_l=cache_v[l], page_table=page_table)
        xs = ffn_half(xs, *ffn1_w)
        ys, k_s, v_s, g_s, r_s = token_mixers(rmsnorm(xs, norm_mix[l]), pos_s, attend_s,
                                              state_gla[l], state_ret[l], lam_init, *mix_w)
        xs = ffn_half(xs + ys, *ffn2_w)

        kp_l.append(kp); vp_l.append(vp); gp_l.append(gp); rp_l.append(rp)
        ks_l.append(k_s); vs_l.append(v_s); gs_l.append(g_s); rs_l.append(r_s)

    y_prompt = rmsnorm(xp, norm_final)
    y_sample = rmsnorm(xs, norm_final)
    return (y_prompt, y_sample,
            jnp.stack(kp_l), jnp.stack(vp_l), jnp.stack(gp_l), jnp.stack(rp_l),
            jnp.stack(ks_l), jnp.stack(vs_l), jnp.stack(gs_l), jnp.stack(rs_l))
```

```python
import functools
import math

import jax
import jax.numpy as jnp
import numpy as np
from jax import lax
from jax.experimental import pallas as pl
from jax.experimental.pallas import tpu as pltpu

F32 = jnp.float32
BF16 = jnp.bfloat16

D_MODEL = 1024
N_HEADS = 4
DH_A = 64
D_HEAD = 128
DK = 64
GLA_RANK = 16
GLA_TAU = 16.0
ROPE_BASE = 10000.0
NUM_BUCKETS = 32
MAX_DISTANCE = 128
D_FF = 2816
PAGE_SIZE = 128
CHUNK = 64
EPS = 1e-6
NEG = -1e30
W_BRANCH = N_HEADS * D_HEAD
W_KEYS = N_HEADS * DK

LANES = 128
SUBLANES = 8
VMEM_LIMIT_BYTES = 56 * 1024 * 1024

C_GATES = 0
C_AQ = 3072
C_AK = 3584
C_AV = 4096
C_BQ = 4608
C_BK = 4864
C_BV = 5120
C_BR = 5632
C_CQ = 6144
C_CK = 6400
C_CV = 6656
C_CG = 7168
C_BG = 7680
N_PROJ = 7808

FAR_DISTANCE = 113


def _params(n_axes=1, vmem=VMEM_LIMIT_BYTES):
    return pltpu.CompilerParams(dimension_semantics=("arbitrary",) * n_axes,
                                vmem_limit_bytes=vmem)


def _resident(shape, index_map):
    return pl.BlockSpec(shape, index_map, pipeline_mode=pl.Buffered(1))


def _rms(x, g):
    return x * lax.rsqrt(jnp.mean(x * x, axis=-1, keepdims=True) + EPS) * g


def _sigmoid(x):
    return 1.0 / (1.0 + jnp.exp(-x))


def _dot(a, b):
    return jnp.dot(a, b, preferred_element_type=F32)


def _dot_nt(a, b):
    return lax.dot_general(a, b, (((1,), (1,)), ((), ())), preferred_element_type=F32)


def _dot_tn(a, b, precision=None):
    return lax.dot_general(a, b, (((0,), (0,)), ((), ())), preferred_element_type=F32,
                           precision=precision)


FF_CHUNK = 1408


def _ffn_kernel(x_ref, g_ref, wg_ref, wu_ref, wd_ref, gf_ref, o_ref, *, final_norm):
    x = x_ref[...]
    n = _rms(x, g_ref[...]).astype(BF16)
    acc = None
    for c in range(0, D_FF, FF_CHUNK):
        a = _dot(n, wg_ref[:, c:c + FF_CHUNK])
        u = _dot(n, wu_ref[:, c:c + FF_CHUNK])
        h = (a * _sigmoid(a) * u).astype(BF16)
        part = _dot(h, wd_ref[c:c + FF_CHUNK, :])
        acc = part if acc is None else acc + part
    y = x + 0.5 * acc
    if final_norm:
        y = _rms(y, gf_ref[...])
    o_ref[...] = y


def _ffn_half(x, g, wg, wu, wd, g_final, final_norm):
    m = x.shape[0]
    tm = min(512, m)
    row = lambda i: (i, 0)
    const = lambda i: (0, 0)
    return pl.pallas_call(
        functools.partial(_ffn_kernel, final_norm=final_norm),
        grid=(m // tm,),
        in_specs=[pl.BlockSpec((tm, D_MODEL), row),
                  _resident((1, D_MODEL), const),
                  _resident((D_MODEL, D_FF), const),
                  _resident((D_MODEL, D_FF), const),
                  _resident((D_FF, D_MODEL), const),
                  _resident((1, D_MODEL), const)],
        out_specs=pl.BlockSpec((tm, D_MODEL), row),
        out_shape=jax.ShapeDtypeStruct((m, D_MODEL), F32),
        compiler_params=_params(),
        name="ffn_half",
    )(x, g, wg, wu, wd, g_final)


PROJ_CHUNK = 1024


def _inproj_kernel(x_ref, g_ref, w_ref, p_ref, qkv_ref):
    n = _rms(x_ref[...], g_ref[...]).astype(BF16)
    for c in range(0, N_PROJ, PROJ_CHUNK):
        w = min(PROJ_CHUNK, N_PROJ - c)
        p_ref[:, c:c + w] = _dot(n, w_ref[:, c:c + w])
    qkv_ref[:, 0:W_BRANCH] = (p_ref[:, C_AQ:C_AQ + W_BRANCH] * (DH_A ** -0.5)).astype(BF16)
    qkv_ref[:, W_BRANCH:3 * W_BRANCH] = p_ref[:, C_AK:C_AK + 2 * W_BRANCH].astype(BF16)


def _inproj(x, g, w):
    m = x.shape[0]
    tm = min(256, m)
    row = lambda i: (i, 0)
    const = lambda i: (0, 0)
    return pl.pallas_call(
        _inproj_kernel,
        grid=(m // tm,),
        in_specs=[pl.BlockSpec((tm, D_MODEL), row),
                  _resident((1, D_MODEL), const),
                  _resident((D_MODEL, N_PROJ), const)],
        out_specs=[pl.BlockSpec((tm, N_PROJ), row),
                   pl.BlockSpec((tm, 3 * W_BRANCH), row)],
        out_shape=[jax.ShapeDtypeStruct((m, N_PROJ), F32),
                   jax.ShapeDtypeStruct((m, 3 * W_BRANCH), BF16)],
        compiler_params=_params(),
        name="inproj",
    )(x, g, w)


def _bucket_of(rel):
    n = jnp.maximum(rel, 0)
    max_exact = NUM_BUCKETS // 2
    nf = jnp.maximum(n, max_exact).astype(F32)
    large = max_exact + (jnp.log(nf / max_exact) / math.log(MAX_DISTANCE / max_exact)
                         * (NUM_BUCKETS - max_exact)).astype(jnp.int32)
    large = jnp.minimum(large, NUM_BUCKETS - 1)
    return jnp.where(n < max_exact, n, large)


def _bias_of(bucket, rb_ref, head):
    val = jnp.zeros(bucket.shape, F32)
    for b in range(NUM_BUCKETS):
        val = jnp.where(bucket == b, rb_ref[b, head], val)
    return val


def _bias_kernel(rb_ref, band_ref, dec_ref, *, past, group_width):
    r = lax.broadcasted_iota(jnp.int32, (LANES, LANES), 0)
    c = lax.broadcasted_iota(jnp.int32, (LANES, LANES), 1)
    for d in range(2):
        rel = r - c + d * LANES
        bucket = _bucket_of(rel)
        for h in range(N_HEADS):
            val = _bias_of(bucket, rb_ref, h)
            band_ref[d, h] = jnp.where(rel >= 0, val, NEG)
    rows, width = dec_ref.shape
    row = lax.broadcasted_iota(jnp.int32, (rows, width), 0)
    t = lax.broadcasted_iota(jnp.int32, (rows, width), 1)
    rel = past - ((row >> 3) * group_width + t)
    bucket = _bucket_of(rel)
    head = row & (N_HEADS - 1)
    val = jnp.zeros((rows, width), F32)
    for h in range(N_HEADS):
        val = jnp.where(head == h, _bias_of(bucket, rb_ref, h), val)
    dec_ref[...] = val


def _bias_tables(rel_bias, past, group_width):
    n_groups = past // group_width
    return pl.pallas_call(
        functools.partial(_bias_kernel, past=past, group_width=group_width),
        in_specs=[pl.BlockSpec(memory_space=pltpu.SMEM)],
        out_shape=[jax.ShapeDtypeStruct((2, N_HEADS, LANES, LANES), F32),
                   jax.ShapeDtypeStruct((n_groups * 8, group_width), F32)],
        compiler_params=pltpu.CompilerParams(vmem_limit_bytes=VMEM_LIMIT_BYTES),
        name="bias_tables",
    )(rel_bias)


def _lambda(lq_ref, lam_init):
    lq = lq_ref[...]
    a = jnp.sum(lq[0:1] * lq[1:2], axis=-1, keepdims=True)
    b = jnp.sum(lq[2:3] * lq[3:4], axis=-1, keepdims=True)
    return jnp.exp(a) - jnp.exp(b) + lam_init


def _map_masks(rows):
    lane = lax.broadcasted_iota(jnp.int32, (rows, D_HEAD), 1)
    return lane < DH_A


def _softmax_update(h, s, v_h, m_sc, l_sc, acc_sc):
    m_prev = m_sc[h]
    m_new = jnp.maximum(m_prev, jnp.max(s, axis=-1, keepdims=True))
    alpha = jnp.exp(m_prev - m_new)
    p = jnp.exp(s - m_new)
    l_sc[h] = alpha * l_sc[h] + jnp.sum(p, axis=-1, keepdims=True)
    acc_sc[h] = alpha * acc_sc[h] + _dot(p.astype(BF16), v_h)
    m_sc[h] = m_new


def _attn_kernel(q_ref, k_ref, v_ref, band_ref, rb_ref, lq_ref, na_ref, o_ref,
                 qbd_sc, m_sc, l_sc, acc_sc, *, t, lam_init):
    i = pl.program_id(0)
    nsub = t // LANES
    first_half = _map_masks(t)

    for h in range(N_HEADS):
        q_h = q_ref[:, h * D_HEAD:(h + 1) * D_HEAD]
        zero = jnp.zeros_like(q_h)
        qbd_sc[h, 0:t, :] = jnp.where(first_half, q_h, zero)
        qbd_sc[h, t:2 * t, :] = jnp.where(first_half, zero, q_h)
    m_sc[...] = jnp.full(m_sc.shape, NEG, F32)
    l_sc[...] = jnp.zeros(l_sc.shape, F32)
    acc_sc[...] = jnp.zeros(acc_sc.shape, F32)

    def scores(h, ks):
        k_h = k_ref[pl.ds(ks, t), h * D_HEAD:(h + 1) * D_HEAD]
        v_h = v_ref[pl.ds(ks, t), h * D_HEAD:(h + 1) * D_HEAD]
        return _dot_nt(qbd_sc[h], k_h), v_h

    def add_band(s, h, delta_of):
        far = rb_ref[NUM_BUCKETS - 1, h]
        halves = []
        for mp in range(2):
            rows = []
            for rr in range(nsub):
                cols = []
                for cc in range(nsub):
                    sub = s[mp * t + rr * LANES: mp * t + (rr + 1) * LANES, cc * LANES:(cc + 1) * LANES]
                    delta = delta_of(rr, cc)
                    if delta < 0:
                        cols.append(jnp.full_like(sub, NEG))
                    elif delta < 2:
                        cols.append(sub + band_ref[delta, h])
                    else:
                        cols.append(sub + far)
                rows.append(jnp.concatenate(cols, axis=1) if nsub > 1 else cols[0])
            halves.append(jnp.concatenate(rows, axis=0) if nsub > 1 else rows[0])
        return jnp.concatenate(halves, axis=0)

    def far_block(j, carry):
        ks = pl.multiple_of(j * t, t)
        for h in range(N_HEADS):
            s, v_h = scores(h, ks)
            _softmax_update(h, s + rb_ref[NUM_BUCKETS - 1, h], v_h, m_sc, l_sc, acc_sc)
        return carry

    lax.fori_loop(0, jnp.maximum(i - 1, 0), far_block, 0)

    @pl.when(i >= 1)
    def _():
        ks = pl.multiple_of((i - 1) * t, t)
        for h in range(N_HEADS):
            s, v_h = scores(h, ks)
            s = add_band(s, h, lambda rr, cc: rr - cc + nsub)
            _softmax_update(h, s, v_h, m_sc, l_sc, acc_sc)

    ks = pl.multiple_of(i * t, t)
    lam = _lambda(lq_ref, lam_init)
    for h in range(N_HEADS):
        s, v_h = scores(h, ks)
        s = add_band(s, h, lambda rr, cc: rr - cc)
        _softmax_update(h, s, v_h, m_sc, l_sc, acc_sc)
        o = acc_sc[h] / l_sc[h]
        attn = o[0:t] - lam * o[t:2 * t]
        o_ref[:, h * D_HEAD:(h + 1) * D_HEAD] = (
            _rms(attn, na_ref[...]) * (1.0 - lam_init)).astype(o_ref.dtype)


def _prompt_attention(qkv, band, rel_bias, lambda_qk, norm_a, lam_init):
    seq = qkv.shape[0]
    t = min(512, seq)
    assert t % LANES == 0 and seq % t == 0 and t + 1 >= FAR_DISTANCE
    const2 = lambda i: (0, 0)
    return pl.pallas_call(
        functools.partial(_attn_kernel, t=t, lam_init=lam_init),
        grid=(seq // t,),
        in_specs=[pl.BlockSpec((t, W_BRANCH), lambda i: (i, 0)),
                  _resident((seq, W_BRANCH), lambda i: (0, 1)),
                  _resident((seq, W_BRANCH), lambda i: (0, 2)),
                  _resident((2, N_HEADS, LANES, LANES), lambda i: (0, 0, 0, 0)),
                  pl.BlockSpec(memory_space=pltpu.SMEM),
                  _resident((4, DH_A), const2),
                  _resident((1, D_HEAD), const2)],
        out_specs=pl.BlockSpec((t, W_BRANCH), lambda i: (i, 0)),
        out_shape=jax.ShapeDtypeStruct((seq, W_BRANCH), BF16),
        scratch_shapes=[pltpu.VMEM((N_HEADS, 2 * t, D_HEAD), BF16),
                        pltpu.VMEM((N_HEADS, 2 * t, 1), F32),
                        pltpu.VMEM((N_HEADS, 2 * t, 1), F32),
                        pltpu.VMEM((N_HEADS, 2 * t, D_HEAD), F32)],
        compiler_params=_params(),
        name="prompt_attention",
    )(qkv, qkv, qkv, band, rel_bias, lambda_qk, norm_a)


def _decode_kernel(pt_ref, *refs, n_pages_step, lam_init):
    del pt_ref
    k_refs = refs[:n_pages_step]
    v_refs = refs[n_pages_step:2 * n_pages_step]
    (dec_ref, rb_ref, lq_ref, na_ref, q_ref, ks_ref, vs_ref, o_ref,
     qrows_sc, m_sc, l_sc, acc_sc) = refs[2 * n_pages_step:]
    g = pl.program_id(1)
    n_rows = 2 * N_HEADS
    row = lax.broadcasted_iota(jnp.int32, (n_rows, W_BRANCH), 0)
    lane = lax.broadcasted_iota(jnp.int32, (n_rows, W_BRANCH), 1)
    own_segment = (lane >> 6) == ((row & 3) * 2 + (row >> 2))

    @pl.when(g == 0)
    def _():
        q = q_ref[...] * (DH_A ** -0.5)
        qrows_sc[...] = jnp.where(own_segment, jnp.broadcast_to(q, (n_rows, W_BRANCH)), 0.0)
        m_sc[...] = jnp.full(m_sc.shape, NEG, F32)
        l_sc[...] = jnp.zeros(l_sc.shape, F32)
        acc_sc[...] = jnp.zeros(acc_sc.shape, F32)

    qrows = qrows_sc[...]
    qb = qrows.astype(BF16)
    s = jnp.concatenate([_dot_nt(qb, k_refs[r][...].astype(BF16)) for r in range(n_pages_step)], axis=1)
    s = s + dec_ref[...]
    m_prev = m_sc[...]
    m_new = jnp.maximum(m_prev, jnp.max(s, axis=-1, keepdims=True))
    alpha = jnp.exp(m_prev - m_new)
    p = jnp.exp(s - m_new)
    l_sc[...] = alpha * l_sc[...] + jnp.sum(p, axis=-1, keepdims=True)
    pb = p.astype(BF16)
    pv = None
    for r in range(n_pages_step):
        part = _dot(pb[:, r * PAGE_SIZE:(r + 1) * PAGE_SIZE], v_refs[r][...].astype(BF16))
        pv = part if pv is None else pv + part
    acc_sc[...] = alpha * acc_sc[...] + pv
    m_sc[...] = m_new

    @pl.when(g == pl.num_programs(1) - 1)
    def _():
        head = lax.broadcasted_iota(jnp.int32, (n_rows, 1), 0) & 3
        b0 = jnp.zeros((n_rows, 1), F32)
        for h in range(N_HEADS):
            b0 = jnp.where(head == h, rb_ref[0, h], b0)
        s_own = jnp.sum(qrows * ks_ref[...], axis=-1, keepdims=True) + b0
        m_prev = m_sc[...]
        m_fin = jnp.maximum(m_prev, s_own)
        alpha = jnp.exp(m_prev - m_fin)
        p_own = jnp.exp(s_own - m_fin)
        l_fin = alpha * l_sc[...] + p_own
        acc = alpha * acc_sc[...] + p_own * vs_ref[...]
        o = acc / l_fin
        own_head = (lane >> 7) == (row & 3)
        o = jnp.where(own_head, o, 0.0)
        lam = _lambda(lq_ref, lam_init)
        attn = (jnp.sum(o[0:N_HEADS], axis=0, keepdims=True)
                - lam * jnp.sum(o[N_HEADS:n_rows], axis=0, keepdims=True))
        for h in range(N_HEADS):
            a_h = attn[:, h * D_HEAD:(h + 1) * D_HEAD]
            o_ref[:, h * D_HEAD:(h + 1) * D_HEAD] = (
                _rms(a_h, na_ref[...]) * (1.0 - lam_init)).astype(o_ref.dtype)


def _sample_attention(p_s, cache_k, cache_v, layer, page_table, dec_bias, rel_bias, lambda_qk, norm_a,
                      lam_init, n_pages_step):
    db = p_s.shape[0]
    n_pages = page_table.shape[1]
    n_groups = n_pages // n_pages_step
    gw = n_pages_step * PAGE_SIZE
    p3 = p_s.reshape(db, 1, N_PROJ)
    dec3 = dec_bias.reshape(n_groups, 8, gw)

    def page_spec(r):
        return pl.BlockSpec((None, None, PAGE_SIZE, W_BRANCH),
                            lambda b, g, pt: (layer, pt[b, g * n_pages_step + r], 0, 0))

    def row_spec(col):
        return pl.BlockSpec((None, 1, W_BRANCH), lambda b, g, pt: (b, 0, col // W_BRANCH))

    const2 = lambda b, g, pt: (0, 0)
    grid_spec = pltpu.PrefetchScalarGridSpec(
        num_scalar_prefetch=1,
        grid=(db, n_groups),
        in_specs=([page_spec(r) for r in range(n_pages_step)]
                  + [page_spec(r) for r in range(n_pages_step)]
                  + [pl.BlockSpec((None, 8, gw), lambda b, g, pt: (g, 0, 0)),
                     pl.BlockSpec(memory_space=pltpu.SMEM),
                     pl.BlockSpec((4, DH_A), const2),
                     pl.BlockSpec((1, D_HEAD), const2),
                     row_spec(C_AQ), row_spec(C_AK), row_spec(C_AV)]),
        out_specs=pl.BlockSpec((None, 1, W_BRANCH), lambda b, g, pt: (b, 0, 0)),
        scratch_shapes=[pltpu.VMEM((8, W_BRANCH), F32),
                        pltpu.VMEM((8, 1), F32),
                        pltpu.VMEM((8, 1), F32),
                        pltpu.VMEM((8, W_BRANCH), F32)])
    out = pl.pallas_call(
        functools.partial(_decode_kernel, n_pages_step=n_pages_step, lam_init=lam_init),
        grid_spec=grid_spec,
        out_shape=jax.ShapeDtypeStruct((db, 1, W_BRANCH), BF16),
        compiler_params=_params(2),
        name="sample_attention",
    )(page_table, *([cache_k] * n_pages_step), *([cache_v] * n_pages_step),
      dec3, rel_bias, lambda_qk, norm_a, p3, p3, p3)
    return out.reshape(db, W_BRANCH)


def _log_sigmoid(x):
    return jnp.minimum(x, 0.0) - jnp.log1p(jnp.exp(-jnp.abs(x)))


def _gla_log_decay(bg, wup_ref, bgla_ref):
    x = jnp.dot(bg, wup_ref[...], preferred_element_type=F32, precision=lax.Precision.HIGHEST)
    return _log_sigmoid(x + bgla_ref[...]) / GLA_TAU


def _rotate_half(x):
    width = x.shape[-1]
    lane = lax.broadcasted_iota(jnp.int32, x.shape, x.ndim - 1)
    lo = (lane & (DK - 1)) < DK // 2
    return jnp.where(lo, pltpu.roll(x, width - DK // 2, x.ndim - 1), pltpu.roll(x, DK // 2, x.ndim - 1))


def _rotary(x, cos, sin_signed):
    return x * cos + _rotate_half(x) * sin_signed


def _log_gamma(h):
    return math.log(1.0 - 2.0 ** (-5.0 - h))


def _head_of_keys(shape, dim):
    return lax.broadcasted_iota(jnp.int32, shape, dim) >> 6


def _layernorm(x, g):
    xc = x - jnp.mean(x, axis=-1, keepdims=True)
    return xc * lax.rsqrt(jnp.mean(xc * xc, axis=-1, keepdims=True) + EPS) * g


def _block_diag_mask():
    r = lax.broadcasted_iota(jnp.int32, (W_KEYS, W_BRANCH), 0) >> 6
    c = lax.broadcasted_iota(jnp.int32, (W_KEYS, W_BRANCH), 1) >> 7
    return r == c


def _linattn_kernel(bq_ref, bk_ref, bv_ref, bg_ref, cq_ref, ck_ref, cv_ref, cos_ref, sin_ref,
                    wup_ref, bgla_ref, nb_ref, nc_ref,
                    bo_ref, co_ref, sg_ref, sr_ref,
                    stg_sc, str_sc, *, rows):
    step = pl.program_id(0)
    hi = lax.Precision.HIGHEST

    @pl.when(step == 0)
    def _():
        stg_sc[...] = jnp.zeros(stg_sc.shape, F32)
        str_sc[...] = jnp.zeros(str_sc.shape, F32)

    bd = _block_diag_mask()
    ti = lax.broadcasted_iota(jnp.int32, (CHUNK, CHUNK), 0)
    si = lax.broadcasted_iota(jnp.int32, (CHUNK, CHUNK), 1)
    causal = ti >= si
    tril = jnp.where(causal, 1.0, 0.0).astype(F32)
    ones_cols = jnp.ones((CHUNK, D_HEAD), F32)
    key_head = _head_of_keys((CHUNK, W_KEYS), 1)

    tcol = lax.broadcasted_iota(jnp.int32, (CHUNK, W_KEYS), 0).astype(F32)
    lg_lane = jnp.zeros((CHUNK, W_KEYS), F32)
    lg_row = jnp.zeros((W_KEYS, W_BRANCH), F32)
    row_head = _head_of_keys((W_KEYS, W_BRANCH), 0)
    for h in range(N_HEADS):
        lg_lane = jnp.where(key_head == h, _log_gamma(h), lg_lane)
        lg_row = jnp.where(row_head == h, _log_gamma(h), lg_row)
    g_fwd = jnp.exp((tcol + 1.0) * lg_lane)
    g_rev = jnp.exp((CHUNK - 1.0 - tcol) * lg_lane)
    g_end = jnp.exp(CHUNK * lg_row)
    dist = (ti - si).astype(F32)

    glog = _gla_log_decay(bg_ref[...], wup_ref, bgla_ref)
    cq_all = _rotary(cq_ref[...], cos_ref[...], sin_ref[...])
    ck_all = _rotary(ck_ref[...], cos_ref[...], sin_ref[...]) * (DK ** -0.5)
    bq_all = bq_ref[...] * (DK ** -0.5)

    for c in range(rows // CHUNK):
        rs = slice(c * CHUNK, (c + 1) * CHUNK)
        gi = glog[rs]
        gcum = jnp.dot(tril, gi, preferred_element_type=F32, precision=hi)
        gend_row = gcum[CHUNK - 1:CHUNK]
        gend_col = _dot_tn(gi, ones_cols, precision=hi)
        qd = bq_all[rs] * jnp.exp(gcum)
        kinv = (bk_ref[rs, :] * jnp.exp(-gcum)).astype(BF16)
        krem = (bk_ref[rs, :] * jnp.exp(gend_row - gcum)).astype(BF16)
        v = bv_ref[rs, :].astype(BF16)
        st = stg_sc[...]
        o = _dot(qd.astype(BF16), st.astype(BF16))
        parts = []
        for h in range(N_HEADS):
            qm = jnp.where(key_head == h, qd, 0.0).astype(BF16)
            sc = jnp.where(causal, _dot_nt(qm, kinv), 0.0)
            parts.append(_dot(sc.astype(BF16), v[:, h * D_HEAD:(h + 1) * D_HEAD]))
        o = o + jnp.concatenate(parts, axis=1)
        upd = jnp.where(bd, _dot_tn(krem, v), 0.0)
        e_end = jnp.exp(gend_col)
        stg_sc[...] = jnp.concatenate([e_end] * N_HEADS, axis=1) * st + upd
        for h in range(N_HEADS):
            hs = slice(h * D_HEAD, (h + 1) * D_HEAD)
            bo_ref[rs, hs] = _rms(o[:, hs], nb_ref[:, hs]).astype(bo_ref.dtype)
        q = cq_all[rs]
        k = ck_all[rs]
        v = cv_ref[rs, :].astype(BF16)
        st = str_sc[...]
        o = _dot((q * g_fwd).astype(BF16), st.astype(BF16))
        parts = []
        kb = k.astype(BF16)
        for h in range(N_HEADS):
            qm = jnp.where(key_head == h, q, 0.0).astype(BF16)
            decay = jnp.where(causal, jnp.exp(dist * _log_gamma(h)), 0.0)
            sc = _dot_nt(qm, kb) * decay
            parts.append(_dot(sc.astype(BF16), v[:, h * D_HEAD:(h + 1) * D_HEAD]))
        o = o + jnp.concatenate(parts, axis=1)
        upd = jnp.where(bd, _dot_tn((k * g_rev).astype(BF16), v), 0.0)
        str_sc[...] = g_end * st + upd
        for h in range(N_HEADS):
            hs = slice(h * D_HEAD, (h + 1) * D_HEAD)
            co_ref[rs, hs] = _layernorm(o[:, hs], nc_ref[:, hs]).astype(co_ref.dtype)

    @pl.when(step == pl.num_programs(0) - 1)
    def _():
        for h in range(N_HEADS):
            sg_ref[h] = stg_sc[h * DK:(h + 1) * DK, h * D_HEAD:(h + 1) * D_HEAD]
            sr_ref[h] = str_sc[h * DK:(h + 1) * DK, h * D_HEAD:(h + 1) * D_HEAD]


def _prompt_linear_attention(p, cos, sin, wup, bgla, norm_b, norm_c):
    seq = p.shape[0]
    rows = min(512, seq)

    def col(c, w):
        return pl.BlockSpec((rows, w), lambda i: (i, c // w))

    const = lambda i: (0, 0)
    state = jax.ShapeDtypeStruct((N_HEADS, DK, D_HEAD), F32)
    state_spec = pl.BlockSpec((N_HEADS, DK, D_HEAD), lambda i: (0, 0, 0))
    return pl.pallas_call(
        functools.partial(_linattn_kernel, rows=rows),
        grid=(seq // rows,),
        in_specs=[col(C_BQ, W_KEYS), col(C_BK, W_KEYS), col(C_BV, W_BRANCH), col(C_BG, LANES),
                  col(C_CQ, W_KEYS), col(C_CK, W_KEYS), col(C_CV, W_BRANCH),
                  pl.BlockSpec((rows, W_KEYS), lambda i: (i, 0)),
                  pl.BlockSpec((rows, W_KEYS), lambda i: (i, 0)),
                  pl.BlockSpec((LANES, W_KEYS), const),
                  pl.BlockSpec((1, W_KEYS), const),
                  pl.BlockSpec((1, W_BRANCH), const),
                  pl.BlockSpec((1, W_BRANCH), const)],
        out_specs=[pl.BlockSpec((rows, W_BRANCH), lambda i: (i, 0)),
                   pl.BlockSpec((rows, W_BRANCH), lambda i: (i, 0)),
                   state_spec, state_spec],
        out_shape=[jax.ShapeDtypeStruct((seq, W_BRANCH), BF16),
                   jax.ShapeDtypeStruct((seq, W_BRANCH), BF16),
                   state, state],
        scratch_shapes=[pltpu.VMEM((W_KEYS, W_BRANCH), F32),
                        pltpu.VMEM((W_KEYS, W_BRANCH), F32)],
        compiler_params=_params(),
        name="prompt_linear_attention",
    )(p, p, p, p, p, p, p, cos, sin, wup, bgla, norm_b, norm_c)


def _to_column(x_row):
    n = x_row.shape[-1]
    r = lax.broadcasted_iota(jnp.int32, (n, n), 0)
    c = lax.broadcasted_iota(jnp.int32, (n, n), 1)
    return jnp.sum(jnp.where(r == c, jnp.broadcast_to(x_row, (n, n)), 0.0), axis=-1, keepdims=True)


def _recurrent_kernel(bq_ref, bk_ref, bv_ref, bg_ref, cq_ref, ck_ref, cv_ref, cos_ref, sin_ref,
                      wup_ref, bgla_ref, nb_ref, nc_ref, sg0_ref, sr0_ref,
                      bo_ref, co_ref, sg_ref, sr_ref):
    decay = _to_column(jnp.exp(_gla_log_decay(bg_ref[...], wup_ref, bgla_ref)))
    bq = _to_column(bq_ref[...] * (DK ** -0.5))
    bk = _to_column(bk_ref[...])
    cq = _to_column(_rotary(cq_ref[...], cos_ref[...], sin_ref[...]))
    ck = _to_column(_rotary(ck_ref[...], cos_ref[...], sin_ref[...]) * (DK ** -0.5))
    for h in range(N_HEADS):
        ks = slice(h * DK, (h + 1) * DK)
        hs = slice(h * D_HEAD, (h + 1) * D_HEAD)
        s_new = decay[ks] * sg0_ref[h] + bk[ks] * bv_ref[:, hs]
        sg_ref[h] = s_new
        o = jnp.sum(bq[ks] * s_new, axis=0, keepdims=True)
        bo_ref[:, hs] = _rms(o, nb_ref[:, hs]).astype(bo_ref.dtype)
        s_new = math.exp(_log_gamma(h)) * sr0_ref[h] + ck[ks] * cv_ref[:, hs]
        sr_ref[h] = s_new
        o = jnp.sum(cq[ks] * s_new, axis=0, keepdims=True)
        co_ref[:, hs] = _layernorm(o, nc_ref[:, hs]).astype(co_ref.dtype)


def _sample_linear_attention(p_s, cos, sin, wup, bgla, norm_b, norm_c, state_gla, state_ret, layer):
    db = p_s.shape[0]
    p3 = p_s.reshape(db, 1, N_PROJ)

    def col(c, w):
        return pl.BlockSpec((None, 1, w), lambda b: (b, 0, c // w))

    const = lambda b: (0, 0)
    state_spec = pl.BlockSpec((None, N_HEADS, DK, D_HEAD), lambda b: (b, 0, 0, 0))
    state_in_spec = pl.BlockSpec((None, None, N_HEADS, DK, D_HEAD), lambda b: (layer, b, 0, 0, 0))
    out_row = pl.BlockSpec((None, 1, W_BRANCH), lambda b: (b, 0, 0))
    state = jax.ShapeDtypeStruct((db, N_HEADS, DK, D_HEAD), F32)
    bo, co, sg, sr = pl.pallas_call(
        _recurrent_kernel,
        grid=(db,),
        in_specs=[col(C_BQ, W_KEYS), col(C_BK, W_KEYS), col(C_BV, W_BRANCH), col(C_BG, LANES),
                  col(C_CQ, W_KEYS), col(C_CK, W_KEYS), col(C_CV, W_BRANCH),
                  pl.BlockSpec((1, W_KEYS), const), pl.BlockSpec((1, W_KEYS), const),
                  pl.BlockSpec((LANES, W_KEYS), const),
                  pl.BlockSpec((1, W_KEYS), const),
                  pl.BlockSpec((1, W_BRANCH), const),
                  pl.BlockSpec((1, W_BRANCH), const),
                  state_in_spec, state_in_spec],
        out_specs=[out_row, out_row, state_spec, state_spec],
        out_shape=[jax.ShapeDtypeStruct((db, 1, W_BRANCH), BF16),
                   jax.ShapeDtypeStruct((db, 1, W_BRANCH), BF16),
                   state, state],
        compiler_params=_params(),
        name="sample_linear_attention",
    )(p3, p3, p3, p3, p3, p3, p3, cos, sin, wup, bgla, norm_b, norm_c, state_gla, state_ret)
    return bo.reshape(db, W_BRANCH), co.reshape(db, W_BRANCH), sg, sr


def _merge_kernel(x_ref, ao_ref, bo_ref, co_ref, br_ref, cg_ref, gates_ref,
                  wa_ref, wb_ref, wc_ref, wo_ref, o_ref):
    ya = _dot(ao_ref[...], wa_ref[...])
    br = br_ref[...]
    yb = _dot((bo_ref[...].astype(F32) * (br * _sigmoid(br))).astype(BF16), wb_ref[...])
    cg = cg_ref[...]
    yc = _dot((co_ref[...].astype(F32) * (cg * _sigmoid(cg))).astype(BF16), wc_ref[...])
    mix = (_sigmoid(gates_ref[:, 0:D_MODEL]) * ya
           + _sigmoid(gates_ref[:, D_MODEL:2 * D_MODEL]) * yb
           + _sigmoid(gates_ref[:, 2 * D_MODEL:3 * D_MODEL]) * yc)
    o_ref[...] = x_ref[...] + _dot(mix.astype(BF16), wo_ref[...])


def _merge(x, ao, bo, co, p, wa, wb, wc, wo):
    m = x.shape[0]
    tm = min(512, m)
    row = lambda i: (i, 0)
    const = lambda i: (0, 0)
    return pl.pallas_call(
        _merge_kernel,
        grid=(m // tm,),
        in_specs=[pl.BlockSpec((tm, D_MODEL), row),
                  pl.BlockSpec((tm, W_BRANCH), row),
                  pl.BlockSpec((tm, W_BRANCH), row),
                  pl.BlockSpec((tm, W_BRANCH), row),
                  pl.BlockSpec((tm, W_BRANCH), lambda i: (i, C_BR // W_BRANCH)),
                  pl.BlockSpec((tm, W_BRANCH), lambda i: (i, C_CG // W_BRANCH)),
                  pl.BlockSpec((tm, 3 * D_MODEL), lambda i: (i, C_GATES // (3 * D_MODEL))),
                  _resident((W_BRANCH, D_MODEL), const),
                  _resident((W_BRANCH, D_MODEL), const),
                  _resident((W_BRANCH, D_MODEL), const),
                  _resident((D_MODEL, D_MODEL), const)],
        out_specs=pl.BlockSpec((tm, D_MODEL), row),
        out_shape=jax.ShapeDtypeStruct((m, D_MODEL), F32),
        compiler_params=_params(),
        name="merge",
    )(x, ao, bo, co, p, p, p, wa, wb, wc, wo)


def _rope_tables(pos):
    half = DK // 2
    inv = ROPE_BASE ** (-jnp.arange(half, dtype=F32) / half)
    ang = pos.astype(F32)[:, None] * inv[None, :]
    cos = jnp.cos(ang)
    sin = jnp.sin(ang)
    cos_t = jnp.tile(jnp.concatenate([cos, cos], axis=-1), (1, N_HEADS))
    sin_t = jnp.tile(jnp.concatenate([-sin, sin], axis=-1), (1, N_HEADS))
    return cos_t, sin_t


def _reorder_w_in(w):
    s_a = 3 * W_BRANCH
    s_b = s_a + 2 * W_KEYS + 2 * W_BRANCH
    s_g = s_b + GLA_RANK
    s_c = s_g + 2 * W_KEYS + 2 * W_BRANCH
    pad = jnp.zeros((w.shape[0], N_PROJ - C_BG - GLA_RANK), w.dtype)
    return jnp.concatenate([w[:, s_c:], w[:, :s_b], w[:, s_g:s_c], w[:, s_b:s_g], pad], axis=1)


def kernel(x_prompt, x_sample, cache_k, cache_v, state_gla, state_ret, page_table, w_in, w_gla_up, b_gla, lambda_qk, rel_bias, norm_a, norm_b, norm_c, w_a, w_b, w_c, w_out, norm_ffn1, ffn1_wg, ffn1_wu, ffn1_wd, norm_mix, norm_ffn2, ffn2_wg, ffn2_wu, ffn2_wd, norm_final):
    batch, seq, _ = x_prompt.shape
    db, dec_seq, _ = x_sample.shape
    depth = w_in.shape[0]
    assert batch == 1 and dec_seq == 1
    n_pages = page_table.shape[1]
    past = n_pages * PAGE_SIZE
    n_pool = cache_k.shape[1]
    n_pages_step = math.gcd(8, n_pages)

    band, dec_bias = _bias_tables(rel_bias, past, n_pages_step * PAGE_SIZE)
    cos_p, sin_p = _rope_tables(jnp.arange(seq))
    cos_s, sin_s = _rope_tables(past + jnp.arange(dec_seq))
    ck4 = cache_k.reshape(depth, n_pool, PAGE_SIZE, W_BRANCH)
    cv4 = cache_v.reshape(depth, n_pool, PAGE_SIZE, W_BRANCH)
    g_final = norm_final.reshape(1, D_MODEL)

    xp = x_prompt.reshape(seq, D_MODEL)
    xs = x_sample.reshape(db, D_MODEL)
    outs = {k: [] for k in ("kp", "vp", "gp", "rp", "ks", "vs", "gs", "rs")}
    for l in range(depth):
        lam_init = 0.8 - 0.6 * math.exp(-0.3 * l)
        last = l == depth - 1
        w_in_l = _reorder_w_in(w_in[l]).astype(BF16)
        wup = jnp.concatenate([w_gla_up[l], jnp.zeros((LANES - GLA_RANK, W_KEYS), F32)], axis=0)
        bgla = b_gla[l].reshape(1, W_KEYS)
        nb = norm_b[l].reshape(1, W_BRANCH)
        nc = norm_c[l].reshape(1, W_BRANCH)
        na = norm_a[l].reshape(1, D_HEAD)
        ffn1 = (norm_ffn1[l].reshape(1, D_MODEL), ffn1_wg[l].astype(BF16), ffn1_wu[l].astype(BF16),
                ffn1_wd[l].astype(BF16))
        ffn2 = (norm_ffn2[l].reshape(1, D_MODEL), ffn2_wg[l].astype(BF16), ffn2_wu[l].astype(BF16),
                ffn2_wd[l].astype(BF16))
        g_mix = norm_mix[l].reshape(1, D_MODEL)
        proj_w = (w_a[l].astype(BF16), w_b[l].astype(BF16), w_c[l].astype(BF16), w_out[l].astype(BF16))

        xp = _ffn_half(xp, *ffn1, g_final, False)
        p, qkv = _inproj(xp, g_mix, w_in_l)
        ao = _prompt_attention(qkv, band, rel_bias, lambda_qk[l], na, lam_init)
        bo, co, sg, sr = _prompt_linear_attention(p, cos_p, sin_p, wup, bgla, nb, nc)
        xp = _merge(xp, ao, bo, co, p, *proj_w)
        xp = _ffn_half(xp, *ffn2, g_final, last)
        outs["kp"].append(p[:, C_AK:C_AK + W_BRANCH].reshape(batch, seq, N_HEADS, D_HEAD))
        outs["vp"].append(p[:, C_AV:C_AV + W_BRANCH].reshape(batch, seq, N_HEADS, D_HEAD))
        outs["gp"].append(sg[None])
        outs["rp"].append(sr[None])

        xs = _ffn_half(xs, *ffn1, g_final, False)
        p_s, _ = _inproj(xs, g_mix, w_in_l)
        ao = _sample_attention(p_s, ck4, cv4, l, page_table, dec_bias, rel_bias, lambda_qk[l], na,
                               lam_init, n_pages_step)
        bo, co, sg, sr = _sample_linear_attention(p_s, cos_s, sin_s, wup, bgla, nb, nc,
                                                  state_gla, state_ret, l)
        xs = _merge(xs, ao, bo, co, p_s, *proj_w)
        xs = _ffn_half(xs, *ffn2, g_final, last)
        outs["ks"].append(p_s[:, C_AK:C_AK + W_BRANCH].reshape(db, dec_seq, N_HEADS, D_HEAD))
        outs["vs"].append(p_s[:, C_AV:C_AV + W_BRANCH].reshape(db, dec_seq, N_HEADS, D_HEAD))
        outs["gs"].append(sg)
        outs["rs"].append(sr)

    st = lambda key: jnp.stack(outs[key])
    return (xp.reshape(batch, seq, D_MODEL), xs.reshape(db, dec_seq, D_MODEL),
            st("kp"), st("vp"), st("gp"), st("rp"), st("ks"), st("vs"), st("gs"), st("rs"))
```

```python
import functools
import math

import jax
import jax.numpy as jnp
import numpy as np
from jax import lax
from jax.experimental import pallas as pl
from jax.experimental.pallas import tpu as pltpu

F32 = jnp.float32
BF16 = jnp.bfloat16

D_MODEL = 1024
N_HEADS = 4
DH_A = 64
D_HEAD = 128
DK = 64
GLA_RANK = 16
GLA_TAU = 16.0
ROPE_BASE = 10000.0
NUM_BUCKETS = 32
MAX_DISTANCE = 128
D_FF = 2816
PAGE_SIZE = 128
CHUNK = 64
EPS = 1e-6
NEG = -1e30
LOG2E = math.log2(math.e)
W_BRANCH = N_HEADS * D_HEAD
W_KEYS = N_HEADS * DK
PAGE_ROWS = PAGE_SIZE * N_HEADS

LANES = 128
SUBLANES = 8
VMEM_LIMIT_BYTES = 56 * 1024 * 1024

C_GATES = 0
C_AQ = 3072
C_AK = 3584
C_AV = 4096
C_BQ = 4608
C_BK = 4864
C_BV = 5120
C_BR = 5632
C_CQ = 6144
C_CK = 6400
C_CV = 6656
C_CG = 7168
C_BG = 7680
N_PROJ = 7808

FAR_DISTANCE = 113


def _params(n_axes=1, vmem=VMEM_LIMIT_BYTES):
    return pltpu.CompilerParams(dimension_semantics=("arbitrary",) * n_axes,
                                vmem_limit_bytes=vmem)


def _resident(shape, index_map):
    return pl.BlockSpec(shape, index_map, pipeline_mode=pl.Buffered(1))


def _rms(x, g):
    return x * lax.rsqrt(jnp.mean(x * x, axis=-1, keepdims=True) + EPS) * g


def _sigmoid(x):
    return 1.0 / (1.0 + jnp.exp(-x))


def _dot(a, b):
    return jnp.dot(a, b, preferred_element_type=F32)


def _dot_nt(a, b):
    return lax.dot_general(a, b, (((1,), (1,)), ((), ())), preferred_element_type=F32)


def _dot_tn(a, b, precision=None):
    return lax.dot_general(a, b, (((0,), (0,)), ((), ())), preferred_element_type=F32,
                           precision=precision)


FF_CHUNK = 1408


def _ffn_kernel(x_ref, g_ref, wg_ref, wu_ref, wd_ref, gf_ref, o_ref, *, final_norm):
    x = x_ref[...]
    n = _rms(x, g_ref[...]).astype(BF16)
    acc = None
    for c in range(0, D_FF, FF_CHUNK):
        a = _dot(n, wg_ref[:, c:c + FF_CHUNK])
        u = _dot(n, wu_ref[:, c:c + FF_CHUNK])
        h = (a * _sigmoid(a) * u).astype(BF16)
        part = _dot(h, wd_ref[c:c + FF_CHUNK, :])
        acc = part if acc is None else acc + part
    y = x + 0.5 * acc
    if final_norm:
        y = _rms(y, gf_ref[...])
    o_ref[...] = y


def _ffn_half(x, g, wg, wu, wd, g_final, final_norm):
    m = x.shape[0]
    tm = min(512, m)
    row = lambda i: (i, 0)
    const = lambda i: (0, 0)
    return pl.pallas_call(
        functools.partial(_ffn_kernel, final_norm=final_norm),
        grid=(m // tm,),
        in_specs=[pl.BlockSpec((tm, D_MODEL), row),
                  _resident((1, D_MODEL), const),
                  _resident((D_MODEL, D_FF), const),
                  _resident((D_MODEL, D_FF), const),
                  _resident((D_FF, D_MODEL), const),
                  _resident((1, D_MODEL), const)],
        out_specs=pl.BlockSpec((tm, D_MODEL), row),
        out_shape=jax.ShapeDtypeStruct((m, D_MODEL), F32),
        compiler_params=_params(),
        name="ffn_half",
    )(x, g, wg, wu, wd, g_final)


PROJ_CHUNK = 1024


def _inproj_kernel(x_ref, g_ref, w_ref, p_ref, qkv_ref, k_ref, v_ref):
    n = _rms(x_ref[...], g_ref[...]).astype(BF16)
    for c in range(0, N_PROJ, PROJ_CHUNK):
        w = min(PROJ_CHUNK, N_PROJ - c)
        p_ref[:, c:c + w] = _dot(n, w_ref[:, c:c + w])
    qkv_ref[:, 0:W_BRANCH] = (p_ref[:, C_AQ:C_AQ + W_BRANCH] * (DH_A ** -0.5 * LOG2E)).astype(BF16)
    qkv_ref[:, W_BRANCH:3 * W_BRANCH] = p_ref[:, C_AK:C_AK + 2 * W_BRANCH].astype(BF16)
    for h in range(N_HEADS):
        k_ref[:, h, :] = p_ref[:, C_AK + h * D_HEAD:C_AK + (h + 1) * D_HEAD]
        v_ref[:, h, :] = p_ref[:, C_AV + h * D_HEAD:C_AV + (h + 1) * D_HEAD]


def _inproj(x, g, w):
    m = x.shape[0]
    tm = min(256, m)
    row = lambda i: (i, 0)
    const = lambda i: (0, 0)
    rows3 = pl.BlockSpec((tm, N_HEADS, D_HEAD), lambda i: (i, 0, 0))
    return pl.pallas_call(
        _inproj_kernel,
        grid=(m // tm,),
        in_specs=[pl.BlockSpec((tm, D_MODEL), row),
                  _resident((1, D_MODEL), const),
                  _resident((D_MODEL, N_PROJ), const)],
        out_specs=[pl.BlockSpec((tm, N_PROJ), row),
                   pl.BlockSpec((tm, 3 * W_BRANCH), row),
                   rows3, rows3],
        out_shape=[jax.ShapeDtypeStruct((m, N_PROJ), F32),
                   jax.ShapeDtypeStruct((m, 3 * W_BRANCH), BF16),
                   jax.ShapeDtypeStruct((m, N_HEADS, D_HEAD), F32),
                   jax.ShapeDtypeStruct((m, N_HEADS, D_HEAD), F32)],
        compiler_params=_params(),
        name="inproj",
    )(x, g, w)


def _bucket_of(rel):
    n = jnp.maximum(rel, 0)
    max_exact = NUM_BUCKETS // 2
    nf = jnp.maximum(n, max_exact).astype(F32)
    large = max_exact + (jnp.log(nf / max_exact) / math.log(MAX_DISTANCE / max_exact)
                         * (NUM_BUCKETS - max_exact)).astype(jnp.int32)
    large = jnp.minimum(large, NUM_BUCKETS - 1)
    return jnp.where(n < max_exact, n, large)


def _bias_of(bucket, rb_ref, head):
    val = jnp.zeros(bucket.shape, F32)
    for b in range(NUM_BUCKETS):
        val = jnp.where(bucket == b, rb_ref[b, head], val)
    return val


def _bias_kernel(rb_ref, band_ref, dec_ref, *, past, group_width):
    kk = lax.broadcasted_iota(jnp.int32, (LANES, LANES), 0)
    qq = lax.broadcasted_iota(jnp.int32, (LANES, LANES), 1)
    for d in range(2):
        rel = qq - kk + d * LANES
        bucket = _bucket_of(rel)
        for h in range(N_HEADS):
            val = (_bias_of(bucket, rb_ref, h) - rb_ref[NUM_BUCKETS - 1, h]) * LOG2E
            band_ref[d, h] = jnp.where(rel >= 0, val, NEG)
    rows, width = dec_ref.shape
    row = lax.broadcasted_iota(jnp.int32, (rows, width), 0)
    col = lax.broadcasted_iota(jnp.int32, (rows, width), 1)
    rel = past - ((row >> 3) * group_width + (col >> 2))
    bucket = _bucket_of(rel)
    head = col & (N_HEADS - 1)
    own = head == (row & (N_HEADS - 1))
    val = jnp.zeros((rows, width), F32)
    for h in range(N_HEADS):
        val = jnp.where(head == h, _bias_of(bucket, rb_ref, h), val)
    dec_ref[...] = jnp.where(own, val, NEG)


def _bias_tables(rel_bias, past, group_width):
    n_groups = past // group_width
    return pl.pallas_call(
        functools.partial(_bias_kernel, past=past, group_width=group_width),
        in_specs=[pl.BlockSpec(memory_space=pltpu.SMEM)],
        out_shape=[jax.ShapeDtypeStruct((2, N_HEADS, LANES, LANES), F32),
                   jax.ShapeDtypeStruct((n_groups * 8, group_width * N_HEADS), F32)],
        compiler_params=pltpu.CompilerParams(vmem_limit_bytes=VMEM_LIMIT_BYTES),
        name="bias_tables",
    )(rel_bias)


def _lambda(lq_ref, lam_init):
    lq = lq_ref[...]
    a = jnp.sum(lq[0:1] * lq[1:2], axis=-1, keepdims=True)
    b = jnp.sum(lq[2:3] * lq[3:4], axis=-1, keepdims=True)
    return jnp.exp(a) - jnp.exp(b) + lam_init


def _first_map_lanes(rows):
    lane = lax.broadcasted_iota(jnp.int32, (rows, D_HEAD), 1)
    return lane < DH_A


SOFTMAX_STRIP = 2 * LANES


def _softmax_update(h, s, v_h, m_sc, l_sc, acc_sc):
    m_prev = m_sc[h]
    m_new = jnp.maximum(m_prev, jnp.max(s, axis=0, keepdims=True))
    alpha = jnp.exp2(m_prev - m_new)
    p = jnp.exp2(s - m_new)
    l_sc[h] = alpha * l_sc[h] + jnp.sum(p, axis=0, keepdims=True)
    acc_sc[h] = alpha * acc_sc[h] + _dot_tn(v_h, p.astype(BF16))
    m_sc[h] = m_new


def _attn_kernel(q_ref, k_ref, v_ref, band_ref, lq_ref, na_ref, o_ref,
                 qbd_sc, m_sc, l_sc, acc_sc, s_sc, p_sc, a_sc, *, t, lam_init):
    i = pl.program_id(0)
    nsub = t // LANES
    first_map = _first_map_lanes(t)

    for h in range(N_HEADS):
        q_h = q_ref[:, h * D_HEAD:(h + 1) * D_HEAD]
        zero = jnp.zeros_like(q_h)
        qbd_sc[h, 0:t, :] = jnp.where(first_map, q_h, zero)
        qbd_sc[h, t:2 * t, :] = jnp.where(first_map, zero, q_h)
    m_sc[...] = jnp.full(m_sc.shape, NEG, F32)
    l_sc[...] = jnp.zeros(l_sc.shape, F32)
    acc_sc[...] = jnp.zeros(acc_sc.shape, F32)

    def scores(h, ks):
        k_h = k_ref[pl.ds(ks, t), h * D_HEAD:(h + 1) * D_HEAD]
        v_h = v_ref[pl.ds(ks, t), h * D_HEAD:(h + 1) * D_HEAD]
        return _dot_nt(k_h, qbd_sc[h]), v_h

    def add_band(s, h, lead_of):
        rows = []
        for kc in range(nsub):
            cols = []
            for mp in range(2):
                for qr in range(nsub):
                    sub = s[kc * LANES:(kc + 1) * LANES, mp * t + qr * LANES: mp * t + (qr + 1) * LANES]
                    lead = lead_of(kc, qr)
                    if lead < 0:
                        cols.append(jnp.full_like(sub, NEG))
                    elif lead < 2:
                        cols.append(sub + band_ref[lead, h])
                    else:
                        cols.append(sub)
            rows.append(jnp.concatenate(cols, axis=1))
        return jnp.concatenate(rows, axis=0) if nsub > 1 else rows[0]

    def stage_scores(h, ks, slot):
        s_sc[slot] = _dot_nt(k_ref[pl.ds(ks, t), h * D_HEAD:(h + 1) * D_HEAD], qbd_sc[h])

    def stage_softmax(h, slot):
        m_prev = m_sc[h]
        m_new = jnp.maximum(m_prev, jnp.max(s_sc[slot], axis=0, keepdims=True))
        alpha = jnp.exp2(m_prev - m_new)
        sums = []
        for c0 in range(0, 2 * t, SOFTMAX_STRIP):
            cs = slice(c0, c0 + SOFTMAX_STRIP)
            p = jnp.exp2(s_sc[slot, :, cs] - m_new[:, cs])
            sums.append(jnp.sum(p, axis=0, keepdims=True))
            p_sc[slot, :, cs] = p.astype(BF16)
        l_sc[h] = alpha * l_sc[h] + jnp.concatenate(sums, axis=1)
        a_sc[h] = alpha
        m_sc[h] = m_new

    def stage_values(h, ks, slot):
        v_h = v_ref[pl.ds(ks, t), h * D_HEAD:(h + 1) * D_HEAD]
        acc_sc[h] = a_sc[h] * acc_sc[h] + _dot_tn(v_h, p_sc[slot])

    def far_block(j, carry):
        ks = pl.multiple_of(j * t, t)
        stage_scores(0, ks, 0)
        for h in range(N_HEADS):
            if h + 1 < N_HEADS:
                stage_scores(h + 1, ks, (h + 1) % 2)
            stage_softmax(h, h % 2)
            stage_values(h, ks, h % 2)
        return carry

    lax.fori_loop(0, jnp.maximum(i - 1, 0), far_block, 0)

    @pl.when(i >= 1)
    def _():
        ks = pl.multiple_of((i - 1) * t, t)
        for h in range(N_HEADS):
            s, v_h = scores(h, ks)
            s = add_band(s, h, lambda kc, qr: qr - kc + nsub)
            _softmax_update(h, s, v_h, m_sc, l_sc, acc_sc)

    ks = pl.multiple_of(i * t, t)
    lam = _lambda(lq_ref, lam_init)
    for h in range(N_HEADS):
        s, v_h = scores(h, ks)
        s = add_band(s, h, lambda kc, qr: qr - kc)
        _softmax_update(h, s, v_h, m_sc, l_sc, acc_sc)
        o = acc_sc[h] / l_sc[h]
        attn = o[:, 0:t] - lam * o[:, t:2 * t]
        norm = lax.rsqrt(jnp.mean(attn * attn, axis=0, keepdims=True) + EPS)
        o_ref[h * D_HEAD:(h + 1) * D_HEAD, :] = (
            attn * norm * na_ref[...] * (1.0 - lam_init)).astype(o_ref.dtype)


def _prompt_attention(qkv, band, lambda_qk, norm_a_col, lam_init):
    seq = qkv.shape[0]
    t = min(512, seq)
    assert t % LANES == 0 and seq % t == 0 and t + 1 >= FAR_DISTANCE
    const2 = lambda i: (0, 0)
    return pl.pallas_call(
        functools.partial(_attn_kernel, t=t, lam_init=lam_init),
        grid=(seq // t,),
        in_specs=[pl.BlockSpec((t, W_BRANCH), lambda i: (i, 0)),
                  _resident((seq, W_BRANCH), lambda i: (0, 1)),
                  _resident((seq, W_BRANCH), lambda i: (0, 2)),
                  _resident((2, N_HEADS, LANES, LANES), lambda i: (0, 0, 0, 0)),
                  _resident((4, DH_A), const2),
                  _resident((D_HEAD, 1), const2)],
        out_specs=pl.BlockSpec((W_BRANCH, t), lambda i: (0, i)),
        out_shape=jax.ShapeDtypeStruct((W_BRANCH, seq), BF16),
        scratch_shapes=[pltpu.VMEM((N_HEADS, 2 * t, D_HEAD), BF16),
                        pltpu.VMEM((N_HEADS, 1, 2 * t), F32),
                        pltpu.VMEM((N_HEADS, 1, 2 * t), F32),
                        pltpu.VMEM((N_HEADS, D_HEAD, 2 * t), F32),
                        pltpu.VMEM((2, t, 2 * t), F32),
                        pltpu.VMEM((2, t, 2 * t), BF16),
                        pltpu.VMEM((N_HEADS, 1, 2 * t), F32)],
        compiler_params=_params(),
        name="prompt_attention",
    )(qkv, qkv, qkv, band, lambda_qk, norm_a_col)


DECODE_PAGES_PER_STEP = 16


def _rows_by_head(x_row):
    pieces = [x_row[:, h * D_HEAD:(h + 1) * D_HEAD] for h in range(N_HEADS)]
    return jnp.concatenate(pieces + pieces, axis=0)


def _decode_kernel(pt_ref, *refs, n_pages_step, lam_init):
    del pt_ref
    k_refs = refs[:n_pages_step]
    v_refs = refs[n_pages_step:2 * n_pages_step]
    (dec_ref, rb_ref, lq_ref, na_ref, q_ref, ks_ref, vs_ref, o_ref,
     qrows_sc, m_sc, l_sc, acc_sc) = refs[2 * n_pages_step:]
    g = pl.program_id(1)
    n_rows = 2 * N_HEADS
    row = lax.broadcasted_iota(jnp.int32, (n_rows, D_HEAD), 0)
    lane = lax.broadcasted_iota(jnp.int32, (n_rows, D_HEAD), 1)
    own_map = (lane >> 6) == (row >> 2)

    @pl.when(g == 0)
    def _():
        q8 = _rows_by_head(q_ref[...] * (DH_A ** -0.5))
        qrows_sc[...] = jnp.where(own_map, q8, 0.0)
        m_sc[...] = jnp.full(m_sc.shape, NEG, F32)
        l_sc[...] = jnp.zeros(l_sc.shape, F32)
        acc_sc[...] = jnp.zeros(acc_sc.shape, F32)

    qrows = qrows_sc[...]
    qb = qrows.astype(BF16)
    s = jnp.concatenate([_dot_nt(qb, k_refs[r][...].astype(BF16)) for r in range(n_pages_step)], axis=1)
    s = s + dec_ref[...]
    m_prev = m_sc[...]
    m_new = jnp.maximum(m_prev, jnp.max(s, axis=-1, keepdims=True))
    alpha = jnp.exp(m_prev - m_new)
    p = jnp.exp(s - m_new)
    l_sc[...] = alpha * l_sc[...] + jnp.sum(p, axis=-1, keepdims=True)
    pb = p.astype(BF16)
    pv = None
    for r in range(n_pages_step):
        part = _dot(pb[:, r * PAGE_ROWS:(r + 1) * PAGE_ROWS], v_refs[r][...].astype(BF16))
        pv = part if pv is None else pv + part
    acc_sc[...] = alpha * acc_sc[...] + pv
    m_sc[...] = m_new

    @pl.when(g == pl.num_programs(1) - 1)
    def _():
        head = lax.broadcasted_iota(jnp.int32, (n_rows, 1), 0) & (N_HEADS - 1)
        b0 = jnp.zeros((n_rows, 1), F32)
        for h in range(N_HEADS):
            b0 = jnp.where(head == h, rb_ref[0, h], b0)
        s_own = jnp.sum(qrows * _rows_by_head(ks_ref[...]), axis=-1, keepdims=True) + b0
        m_prev = m_sc[...]
        m_fin = jnp.maximum(m_prev, s_own)
        alpha = jnp.exp(m_prev - m_fin)
        p_own = jnp.exp(s_own - m_fin)
        l_fin = alpha * l_sc[...] + p_own
        acc = alpha * acc_sc[...] + p_own * _rows_by_head(vs_ref[...])
        o = acc / l_fin
        lam = _lambda(lq_ref, lam_init)
        attn = o[0:N_HEADS] - lam * o[N_HEADS:n_rows]
        out = _rms(attn, na_ref[...]) * (1.0 - lam_init)
        for h in range(N_HEADS):
            o_ref[:, h * D_HEAD:(h + 1) * D_HEAD] = out[h:h + 1].astype(o_ref.dtype)


def _sample_attention(p_s, cache_k, cache_v, layer, page_table, dec_bias, rel_bias, lambda_qk, norm_a,
                      lam_init, n_pages_step):
    db = p_s.shape[0]
    n_pages = page_table.shape[1]
    n_groups = n_pages // n_pages_step
    gw = n_pages_step * PAGE_ROWS
    p3 = p_s.reshape(db, 1, N_PROJ)
    dec3 = dec_bias.reshape(n_groups, 8, gw)

    def page_spec(r):
        return pl.BlockSpec((None, None, PAGE_ROWS, D_HEAD),
                            lambda b, g, pt: (layer, pt[b, g * n_pages_step + r], 0, 0))

    def row_spec(col):
        return pl.BlockSpec((None, 1, W_BRANCH), lambda b, g, pt: (b, 0, col // W_BRANCH))

    const2 = lambda b, g, pt: (0, 0)
    grid_spec = pltpu.PrefetchScalarGridSpec(
        num_scalar_prefetch=1,
        grid=(db, n_groups),
        in_specs=([page_spec(r) for r in range(n_pages_step)]
                  + [page_spec(r) for r in range(n_pages_step)]
                  + [pl.BlockSpec((None, 8, gw), lambda b, g, pt: (g, 0, 0)),
                     pl.BlockSpec(memory_space=pltpu.SMEM),
                     pl.BlockSpec((4, DH_A), const2),
                     pl.BlockSpec((1, D_HEAD), const2),
                     row_spec(C_AQ), row_spec(C_AK), row_spec(C_AV)]),
        out_specs=pl.BlockSpec((None, 1, W_BRANCH), lambda b, g, pt: (b, 0, 0)),
        scratch_shapes=[pltpu.VMEM((8, D_HEAD), F32),
                        pltpu.VMEM((8, 1), F32),
                        pltpu.VMEM((8, 1), F32),
                        pltpu.VMEM((8, D_HEAD), F32)])
    out = pl.pallas_call(
        functools.partial(_decode_kernel, n_pages_step=n_pages_step, lam_init=lam_init),
        grid_spec=grid_spec,
        out_shape=jax.ShapeDtypeStruct((db, 1, W_BRANCH), BF16),
        compiler_params=_params(2),
        name="sample_attention",
    )(page_table, *([cache_k] * n_pages_step), *([cache_v] * n_pages_step),
      dec3, rel_bias, lambda_qk, norm_a, p3, p3, p3)
    return out.reshape(db, W_BRANCH)


def _log_sigmoid(x):
    return jnp.minimum(x, 0.0) - jnp.log1p(jnp.exp(-jnp.abs(x)))


def _gla_log_decay(bg, wup_ref, bgla_ref):
    x = jnp.dot(bg, wup_ref[...], preferred_element_type=F32, precision=lax.Precision.HIGHEST)
    return _log_sigmoid(x + bgla_ref[...]) / GLA_TAU


def _rotate_half(x):
    width = x.shape[-1]
    lane = lax.broadcasted_iota(jnp.int32, x.shape, x.ndim - 1)
    lo = (lane & (DK - 1)) < DK // 2
    return jnp.where(lo, pltpu.roll(x, width - DK // 2, x.ndim - 1), pltpu.roll(x, DK // 2, x.ndim - 1))


def _rotary(x, cos, sin_signed):
    return x * cos + _rotate_half(x) * sin_signed


def _log_gamma(h):
    return math.log(1.0 - 2.0 ** (-5.0 - h))


def _head_of_keys(shape, dim):
    return lax.broadcasted_iota(jnp.int32, shape, dim) >> 6


def _layernorm(x, g):
    xc = x - jnp.mean(x, axis=-1, keepdims=True)
    return xc * lax.rsqrt(jnp.mean(xc * xc, axis=-1, keepdims=True) + EPS) * g


def _block_diag_mask():
    r = lax.broadcasted_iota(jnp.int32, (W_KEYS, W_BRANCH), 0) >> 6
    c = lax.broadcasted_iota(jnp.int32, (W_KEYS, W_BRANCH), 1) >> 7
    return r == c


def _linattn_kernel(bq_ref, bk_ref, bv_ref, bg_ref, cq_ref, ck_ref, cv_ref, cos_ref, sin_ref,
                    wup_ref, bgla_ref, nb_ref, nc_ref,
                    bo_ref, co_ref, sg_ref, sr_ref,
                    stg_sc, str_sc, *, rows):
    step = pl.program_id(0)
    hi = lax.Precision.HIGHEST

    @pl.when(step == 0)
    def _():
        stg_sc[...] = jnp.zeros(stg_sc.shape, F32)
        str_sc[...] = jnp.zeros(str_sc.shape, F32)

    bd = _block_diag_mask()
    ti = lax.broadcasted_iota(jnp.int32, (CHUNK, CHUNK), 0)
    si = lax.broadcasted_iota(jnp.int32, (CHUNK, CHUNK), 1)
    causal = ti >= si
    tril = jnp.where(causal, 1.0, 0.0).astype(F32)
    ones_cols = jnp.ones((CHUNK, D_HEAD), F32)
    key_head = _head_of_keys((CHUNK, W_KEYS), 1)

    tcol = lax.broadcasted_iota(jnp.int32, (CHUNK, W_KEYS), 0).astype(F32)
    lg_lane = jnp.zeros((CHUNK, W_KEYS), F32)
    lg_row = jnp.zeros((W_KEYS, W_BRANCH), F32)
    row_head = _head_of_keys((W_KEYS, W_BRANCH), 0)
    for h in range(N_HEADS):
        lg_lane = jnp.where(key_head == h, _log_gamma(h), lg_lane)
        lg_row = jnp.where(row_head == h, _log_gamma(h), lg_row)
    g_fwd = jnp.exp((tcol + 1.0) * lg_lane)
    g_rev = jnp.exp((CHUNK - 1.0 - tcol) * lg_lane)
    g_end = jnp.exp(CHUNK * lg_row)
    dist = (ti - si).astype(F32)

    glog = _gla_log_decay(bg_ref[...], wup_ref, bgla_ref)
    cq_all = _rotary(cq_ref[...], cos_ref[...], sin_ref[...])
    ck_all = _rotary(ck_ref[...], cos_ref[...], sin_ref[...]) * (DK ** -0.5)
    bq_all = bq_ref[...] * (DK ** -0.5)

    for c in range(rows // CHUNK):
        rs = slice(c * CHUNK, (c + 1) * CHUNK)
        gi = glog[rs]
        gcum = jnp.dot(tril, gi, preferred_element_type=F32, precision=hi)
        gend_row = gcum[CHUNK - 1:CHUNK]
        gend_col = _dot_tn(gi, ones_cols, precision=hi)
        qd = bq_all[rs] * jnp.exp(gcum)
        kinv = (bk_ref[rs, :] * jnp.exp(-gcum)).astype(BF16)
        krem = (bk_ref[rs, :] * jnp.exp(gend_row - gcum)).astype(BF16)
        v = bv_ref[rs, :].astype(BF16)
        st = stg_sc[...]
        o = _dot(qd.astype(BF16), st.astype(BF16))
        parts = []
        for h in range(N_HEADS):
            qm = jnp.where(key_head == h, qd, 0.0).astype(BF16)
            sc = jnp.where(causal, _dot_nt(qm, kinv), 0.0)
            parts.append(_dot(sc.astype(BF16), v[:, h * D_HEAD:(h + 1) * D_HEAD]))
        o = o + jnp.concatenate(parts, axis=1)
        upd = jnp.where(bd, _dot_tn(krem, v), 0.0)
        e_end = jnp.exp(gend_col)
        stg_sc[...] = jnp.concatenate([e_end] * N_HEADS, axis=1) * st + upd
        for h in range(N_HEADS):
            hs = slice(h * D_HEAD, (h + 1) * D_HEAD)
            bo_ref[rs, hs] = _rms(o[:, hs], nb_ref[:, hs]).astype(bo_ref.dtype)
        q = cq_all[rs]
        k = ck_all[rs]
        v = cv_ref[rs, :].astype(BF16)
        st = str_sc[...]
        o = _dot((q * g_fwd).astype(BF16), st.astype(BF16))
        parts = []
        kb = k.astype(BF16)
        for h in range(N_HEADS):
            qm = jnp.where(key_head == h, q, 0.0).astype(BF16)
            decay = jnp.where(causal, jnp.exp(dist * _log_gamma(h)), 0.0)
            sc = _dot_nt(qm, kb) * decay
            parts.append(_dot(sc.astype(BF16), v[:, h * D_HEAD:(h + 1) * D_HEAD]))
        o = o + jnp.concatenate(parts, axis=1)
        upd = jnp.where(bd, _dot_tn((k * g_rev).astype(BF16), v), 0.0)
        str_sc[...] = g_end * st + upd
        for h in range(N_HEADS):
            hs = slice(h * D_HEAD, (h + 1) * D_HEAD)
            co_ref[rs, hs] = _layernorm(o[:, hs], nc_ref[:, hs]).astype(co_ref.dtype)

    @pl.when(step == pl.num_programs(0) - 1)
    def _():
        for h in range(N_HEADS):
            sg_ref[h] = stg_sc[h * DK:(h + 1) * DK, h * D_HEAD:(h + 1) * D_HEAD]
            sr_ref[h] = str_sc[h * DK:(h + 1) * DK, h * D_HEAD:(h + 1) * D_HEAD]


def _prompt_linear_attention(p, cos, sin, wup, bgla, norm_b, norm_c):
    seq = p.shape[0]
    rows = min(512, seq)

    def col(c, w):
        return pl.BlockSpec((rows, w), lambda i: (i, c // w))

    const = lambda i: (0, 0)
    state = jax.ShapeDtypeStruct((N_HEADS, DK, D_HEAD), F32)
    state_spec = pl.BlockSpec((N_HEADS, DK, D_HEAD), lambda i: (0, 0, 0))
    return pl.pallas_call(
        functools.partial(_linattn_kernel, rows=rows),
        grid=(seq // rows,),
        in_specs=[col(C_BQ, W_KEYS), col(C_BK, W_KEYS), col(C_BV, W_BRANCH), col(C_BG, LANES),
                  col(C_CQ, W_KEYS), col(C_CK, W_KEYS), col(C_CV, W_BRANCH),
                  pl.BlockSpec((rows, W_KEYS), lambda i: (i, 0)),
                  pl.BlockSpec((rows, W_KEYS), lambda i: (i, 0)),
                  pl.BlockSpec((LANES, W_KEYS), const),
                  pl.BlockSpec((1, W_KEYS), const),
                  pl.BlockSpec((1, W_BRANCH), const),
                  pl.BlockSpec((1, W_BRANCH), const)],
        out_specs=[pl.BlockSpec((rows, W_BRANCH), lambda i: (i, 0)),
                   pl.BlockSpec((rows, W_BRANCH), lambda i: (i, 0)),
                   state_spec, state_spec],
        out_shape=[jax.ShapeDtypeStruct((seq, W_BRANCH), BF16),
                   jax.ShapeDtypeStruct((seq, W_BRANCH), BF16),
                   state, state],
        scratch_shapes=[pltpu.VMEM((W_KEYS, W_BRANCH), F32),
                        pltpu.VMEM((W_KEYS, W_BRANCH), F32)],
        compiler_params=_params(),
        name="prompt_linear_attention",
    )(p, p, p, p, p, p, p, cos, sin, wup, bgla, norm_b, norm_c)


def _to_column(x_row):
    n = x_row.shape[-1]
    r = lax.broadcasted_iota(jnp.int32, (n, n), 0)
    c = lax.broadcasted_iota(jnp.int32, (n, n), 1)
    return jnp.sum(jnp.where(r == c, jnp.broadcast_to(x_row, (n, n)), 0.0), axis=-1, keepdims=True)


def _recurrent_kernel(bq_ref, bk_ref, bv_ref, bg_ref, cq_ref, ck_ref, cv_ref, cos_ref, sin_ref,
                      wup_ref, bgla_ref, nb_ref, nc_ref, sg0_ref, sr0_ref,
                      bo_ref, co_ref, sg_ref, sr_ref):
    decay = _to_column(jnp.exp(_gla_log_decay(bg_ref[...], wup_ref, bgla_ref)))
    bq = _to_column(bq_ref[...] * (DK ** -0.5))
    bk = _to_column(bk_ref[...])
    cq = _to_column(_rotary(cq_ref[...], cos_ref[...], sin_ref[...]))
    ck = _to_column(_rotary(ck_ref[...], cos_ref[...], sin_ref[...]) * (DK ** -0.5))
    for h in range(N_HEADS):
        ks = slice(h * DK, (h + 1) * DK)
        hs = slice(h * D_HEAD, (h + 1) * D_HEAD)
        s_new = decay[ks] * sg0_ref[h] + bk[ks] * bv_ref[:, hs]
        sg_ref[h] = s_new
        o = jnp.sum(bq[ks] * s_new, axis=0, keepdims=True)
        bo_ref[:, hs] = _rms(o, nb_ref[:, hs]).astype(bo_ref.dtype)
        s_new = math.exp(_log_gamma(h)) * sr0_ref[h] + ck[ks] * cv_ref[:, hs]
        sr_ref[h] = s_new
        o = jnp.sum(cq[ks] * s_new, axis=0, keepdims=True)
        co_ref[:, hs] = _layernorm(o, nc_ref[:, hs]).astype(co_ref.dtype)


def _sample_linear_attention(p_s, cos, sin, wup, bgla, norm_b, norm_c, state_gla, state_ret, layer):
    db = p_s.shape[0]
    p3 = p_s.reshape(db, 1, N_PROJ)

    def col(c, w):
        return pl.BlockSpec((None, 1, w), lambda b: (b, 0, c // w))

    const = lambda b: (0, 0)
    state_spec = pl.BlockSpec((None, N_HEADS, DK, D_HEAD), lambda b: (b, 0, 0, 0))
    state_in_spec = pl.BlockSpec((None, None, N_HEADS, DK, D_HEAD), lambda b: (layer, b, 0, 0, 0))
    out_row = pl.BlockSpec((None, 1, W_BRANCH), lambda b: (b, 0, 0))
    state = jax.ShapeDtypeStruct((db, N_HEADS, DK, D_HEAD), F32)
    bo, co, sg, sr = pl.pallas_call(
        _recurrent_kernel,
        grid=(db,),
        in_specs=[col(C_BQ, W_KEYS), col(C_BK, W_KEYS), col(C_BV, W_BRANCH), col(C_BG, LANES),
                  col(C_CQ, W_KEYS), col(C_CK, W_KEYS), col(C_CV, W_BRANCH),
                  pl.BlockSpec((1, W_KEYS), const), pl.BlockSpec((1, W_KEYS), const),
                  pl.BlockSpec((LANES, W_KEYS), const),
                  pl.BlockSpec((1, W_KEYS), const),
                  pl.BlockSpec((1, W_BRANCH), const),
                  pl.BlockSpec((1, W_BRANCH), const),
                  state_in_spec, state_in_spec],
        out_specs=[out_row, out_row, state_spec, state_spec],
        out_shape=[jax.ShapeDtypeStruct((db, 1, W_BRANCH), BF16),
                   jax.ShapeDtypeStruct((db, 1, W_BRANCH), BF16),
                   state, state],
        compiler_params=_params(),
        name="sample_linear_attention",
    )(p3, p3, p3, p3, p3, p3, p3, cos, sin, wup, bgla, norm_b, norm_c, state_gla, state_ret)
    return bo.reshape(db, W_BRANCH), co.reshape(db, W_BRANCH), sg, sr


def _merge_kernel(x_ref, aot_ref, bo_ref, co_ref, br_ref, cg_ref, gates_ref,
                  wa_ref, wb_ref, wc_ref, wo_ref, o_ref):
    ya = _dot_tn(aot_ref[...], wa_ref[...])
    br = br_ref[...]
    yb = _dot((bo_ref[...].astype(F32) * (br * _sigmoid(br))).astype(BF16), wb_ref[...])
    cg = cg_ref[...]
    yc = _dot((co_ref[...].astype(F32) * (cg * _sigmoid(cg))).astype(BF16), wc_ref[...])
    mix = (_sigmoid(gates_ref[:, 0:D_MODEL]) * ya
           + _sigmoid(gates_ref[:, D_MODEL:2 * D_MODEL]) * yb
           + _sigmoid(gates_ref[:, 2 * D_MODEL:3 * D_MODEL]) * yc)
    o_ref[...] = x_ref[...] + _dot(mix.astype(BF16), wo_ref[...])


def _merge(x, aot, bo, co, p, wa, wb, wc, wo):
    m = x.shape[0]
    tm = min(512, m)
    row = lambda i: (i, 0)
    const = lambda i: (0, 0)
    return pl.pallas_call(
        _merge_kernel,
        grid=(m // tm,),
        in_specs=[pl.BlockSpec((tm, D_MODEL), row),
                  pl.BlockSpec((W_BRANCH, tm), lambda i: (0, i)),
                  pl.BlockSpec((tm, W_BRANCH), row),
                  pl.BlockSpec((tm, W_BRANCH), row),
                  pl.BlockSpec((tm, W_BRANCH), lambda i: (i, C_BR // W_BRANCH)),
                  pl.BlockSpec((tm, W_BRANCH), lambda i: (i, C_CG // W_BRANCH)),
                  pl.BlockSpec((tm, 3 * D_MODEL), lambda i: (i, C_GATES // (3 * D_MODEL))),
                  _resident((W_BRANCH, D_MODEL), const),
                  _resident((W_BRANCH, D_MODEL), const),
                  _resident((W_BRANCH, D_MODEL), const),
                  _resident((D_MODEL, D_MODEL), const)],
        out_specs=pl.BlockSpec((tm, D_MODEL), row),
        out_shape=jax.ShapeDtypeStruct((m, D_MODEL), F32),
        compiler_params=_params(),
        name="merge",
    )(x, aot, bo, co, p, p, p, wa, wb, wc, wo)


def _rope_tables(pos):
    half = DK // 2
    inv = ROPE_BASE ** (-jnp.arange(half, dtype=F32) / half)
    ang = pos.astype(F32)[:, None] * inv[None, :]
    cos = jnp.cos(ang)
    sin = jnp.sin(ang)
    cos_t = jnp.tile(jnp.concatenate([cos, cos], axis=-1), (1, N_HEADS))
    sin_t = jnp.tile(jnp.concatenate([-sin, sin], axis=-1), (1, N_HEADS))
    return cos_t, sin_t


def _reorder_w_in(w):
    s_a = 3 * W_BRANCH
    s_b = s_a + 2 * W_KEYS + 2 * W_BRANCH
    s_g = s_b + GLA_RANK
    s_c = s_g + 2 * W_KEYS + 2 * W_BRANCH
    pad = jnp.zeros((w.shape[0], N_PROJ - C_BG - GLA_RANK), w.dtype)
    return jnp.concatenate([w[:, s_c:], w[:, :s_b], w[:, s_g:s_c], w[:, s_b:s_g], pad], axis=1)


def kernel(x_prompt, x_sample, cache_k, cache_v, state_gla, state_ret, page_table, w_in, w_gla_up, b_gla, lambda_qk, rel_bias, norm_a, norm_b, norm_c, w_a, w_b, w_c, w_out, norm_ffn1, ffn1_wg, ffn1_wu, ffn1_wd, norm_mix, norm_ffn2, ffn2_wg, ffn2_wu, ffn2_wd, norm_final):
    batch, seq, _ = x_prompt.shape
    db, dec_seq, _ = x_sample.shape
    depth = w_in.shape[0]
    assert batch == 1 and dec_seq == 1
    n_pages = page_table.shape[1]
    past = n_pages * PAGE_SIZE
    n_pool = cache_k.shape[1]
    n_pages_step = math.gcd(DECODE_PAGES_PER_STEP, n_pages)

    band, dec_bias = _bias_tables(rel_bias, past, n_pages_step * PAGE_SIZE)
    cos_p, sin_p = _rope_tables(jnp.arange(seq))
    cos_s, sin_s = _rope_tables(past + jnp.arange(dec_seq))
    ck4 = cache_k.reshape(depth, n_pool, PAGE_ROWS, D_HEAD)
    cv4 = cache_v.reshape(depth, n_pool, PAGE_ROWS, D_HEAD)
    g_final = norm_final.reshape(1, D_MODEL)

    xp = x_prompt.reshape(seq, D_MODEL)
    xs = x_sample.reshape(db, D_MODEL)
    outs = {k: [] for k in ("kp", "vp", "gp", "rp", "ks", "vs", "gs", "rs")}
    for l in range(depth):
        lam_init = 0.8 - 0.6 * math.exp(-0.3 * l)
        last = l == depth - 1
        w_in_l = _reorder_w_in(w_in[l]).astype(BF16)
        wup = jnp.concatenate([w_gla_up[l], jnp.zeros((LANES - GLA_RANK, W_KEYS), F32)], axis=0)
        bgla = b_gla[l].reshape(1, W_KEYS)
        nb = norm_b[l].reshape(1, W_BRANCH)
        nc = norm_c[l].reshape(1, W_BRANCH)
        na = norm_a[l].reshape(1, D_HEAD)
        na_col = norm_a[l].reshape(D_HEAD, 1)
        ffn1 = (norm_ffn1[l].reshape(1, D_MODEL), ffn1_wg[l].astype(BF16), ffn1_wu[l].astype(BF16),
                ffn1_wd[l].astype(BF16))
        ffn2 = (norm_ffn2[l].reshape(1, D_MODEL), ffn2_wg[l].astype(BF16), ffn2_wu[l].astype(BF16),
                ffn2_wd[l].astype(BF16))
        g_mix = norm_mix[l].reshape(1, D_MODEL)
        proj_w = (w_a[l].astype(BF16), w_b[l].astype(BF16), w_c[l].astype(BF16), w_out[l].astype(BF16))

        xp = _ffn_half(xp, *ffn1, g_final, False)
        p, qkv, k_new, v_new = _inproj(xp, g_mix, w_in_l)
        aot = _prompt_attention(qkv, band, lambda_qk[l], na_col, lam_init)
        bo, co, sg, sr = _prompt_linear_attention(p, cos_p, sin_p, wup, bgla, nb, nc)
        xp = _merge(xp, aot, bo, co, p, *proj_w)
        xp = _ffn_half(xp, *ffn2, g_final, last)
        outs["kp"].append(k_new[None])
        outs["vp"].append(v_new[None])
        outs["gp"].append(sg[None])
        outs["rp"].append(sr[None])

        xs = _ffn_half(xs, *ffn1, g_final, False)
        p_s, _, k_new, v_new = _inproj(xs, g_mix, w_in_l)
        ao = _sample_attention(p_s, ck4, cv4, l, page_table, dec_bias, rel_bias, lambda_qk[l], na,
                               lam_init, n_pages_step)
        bo, co, sg, sr = _sample_linear_attention(p_s, cos_s, sin_s, wup, bgla, nb, nc,
                                                  state_gla, state_ret, l)
        xs = _merge(xs, ao.T, bo, co, p_s, *proj_w)
        xs = _ffn_half(xs, *ffn2, g_final, last)
        outs["ks"].append(k_new[:, None])
        outs["vs"].append(v_new[:, None])
        outs["gs"].append(sg)
        outs["rs"].append(sr)

    st = lambda key: jnp.stack(outs[key])
    return (xp.reshape(batch, seq, D_MODEL), xs.reshape(db, dec_seq, D_MODEL),
            st("kp"), st("vp"), st("gp"), st("rp"), st("ks"), st("vs"), st("gs"), st("rs"))
```

```python
import functools
import math

import jax
import jax.numpy as jnp
import numpy as np
from jax import lax
from jax.experimental import pallas as pl
from jax.experimental.pallas import tpu as pltpu

F32 = jnp.float32
BF16 = jnp.bfloat16

D_MODEL = 1024
N_HEADS = 4
DH_A = 64
D_HEAD = 128
DK = 64
GLA_RANK = 16
GLA_TAU = 16.0
ROPE_BASE = 10000.0
NUM_BUCKETS = 32
MAX_DISTANCE = 128
D_FF = 2816
PAGE_SIZE = 128
CHUNK = 64
EPS = 1e-6
NEG = -1e30
LOG2E = math.log2(math.e)
W_BRANCH = N_HEADS * D_HEAD
W_KEYS = N_HEADS * DK
PAGE_ROWS = PAGE_SIZE * N_HEADS

LANES = 128
SUBLANES = 8
VMEM_LIMIT_BYTES = 56 * 1024 * 1024

C_GATES = 0
C_BR = 3072
C_CG = 3584
N_GATE = 4096
C_AQ = 0
C_AK = 512
C_AV = 1024
C_BQ = 1536
C_BK = 1792
C_BV = 2048
C_CQ = 2560
C_CK = 2816
C_CV = 3072
C_BG = 3584
N_MAIN = 3712
N_PROJ = N_GATE + N_MAIN

FAR_DISTANCE = 113


def _params(n_axes=1, vmem=VMEM_LIMIT_BYTES):
    return pltpu.CompilerParams(dimension_semantics=("arbitrary",) * n_axes,
                                vmem_limit_bytes=vmem)


def _resident(shape, index_map):
    return pl.BlockSpec(shape, index_map, pipeline_mode=pl.Buffered(1))


def _rms(x, g):
    return x * lax.rsqrt(jnp.mean(x * x, axis=-1, keepdims=True) + EPS) * g


def _sigmoid(x):
    return 1.0 / (1.0 + jnp.exp(-x))


def _dot(a, b):
    return jnp.dot(a, b, preferred_element_type=F32)


def _dot_nt(a, b):
    return lax.dot_general(a, b, (((1,), (1,)), ((), ())), preferred_element_type=F32)


def _dot_tn(a, b, precision=None):
    return lax.dot_general(a, b, (((0,), (0,)), ((), ())), preferred_element_type=F32,
                           precision=precision)


FF_CHUNK = 1408


def _ffn_kernel(x_ref, g_ref, wg_ref, wu_ref, wd_ref, gf_ref, o_ref, *, final_norm):
    x = x_ref[...]
    n = _rms(x, g_ref[...]).astype(BF16)
    acc = None
    for c in range(0, D_FF, FF_CHUNK):
        a = _dot(n, wg_ref[:, c:c + FF_CHUNK])
        u = _dot(n, wu_ref[:, c:c + FF_CHUNK])
        h = (a * _sigmoid(a) * u).astype(BF16)
        part = _dot(h, wd_ref[c:c + FF_CHUNK, :])
        acc = part if acc is None else acc + part
    y = x + 0.5 * acc
    if final_norm:
        y = _rms(y, gf_ref[...])
    o_ref[...] = y


def _ffn_half(x, g, wg, wu, wd, g_final, final_norm):
    m = x.shape[0]
    tm = min(512, m)
    row = lambda i: (i, 0)
    const = lambda i: (0, 0)
    return pl.pallas_call(
        functools.partial(_ffn_kernel, final_norm=final_norm),
        grid=(m // tm,),
        in_specs=[pl.BlockSpec((tm, D_MODEL), row),
                  _resident((1, D_MODEL), const),
                  _resident((D_MODEL, D_FF), const),
                  _resident((D_MODEL, D_FF), const),
                  _resident((D_FF, D_MODEL), const),
                  _resident((1, D_MODEL), const)],
        out_specs=pl.BlockSpec((tm, D_MODEL), row),
        out_shape=jax.ShapeDtypeStruct((m, D_MODEL), F32),
        compiler_params=_params(),
        name="ffn_half",
    )(x, g, wg, wu, wd, g_final)


PROJ_CHUNK = 1024


def _inproj_kernel(x_ref, g_ref, w_ref, p_ref, gate_ref, qkv_ref, k_ref, v_ref):
    n = _rms(x_ref[...], g_ref[...]).astype(BF16)
    for c in range(0, N_GATE, PROJ_CHUNK):
        gate_ref[:, c:c + PROJ_CHUNK] = _dot(n, w_ref[:, c:c + PROJ_CHUNK]).astype(BF16)
    for c in range(0, N_MAIN, PROJ_CHUNK):
        w = min(PROJ_CHUNK, N_MAIN - c)
        p_ref[:, c:c + w] = _dot(n, w_ref[:, N_GATE + c:N_GATE + c + w])
    qkv_ref[:, 0:W_BRANCH] = (p_ref[:, C_AQ:C_AQ + W_BRANCH] * (DH_A ** -0.5 * LOG2E)).astype(BF16)
    qkv_ref[:, W_BRANCH:3 * W_BRANCH] = p_ref[:, C_AK:C_AK + 2 * W_BRANCH].astype(BF16)
    for h in range(N_HEADS):
        k_ref[:, h, :] = p_ref[:, C_AK + h * D_HEAD:C_AK + (h + 1) * D_HEAD]
        v_ref[:, h, :] = p_ref[:, C_AV + h * D_HEAD:C_AV + (h + 1) * D_HEAD]


def _inproj(x, g, w):
    m = x.shape[0]
    tm = min(256, m)
    row = lambda i: (i, 0)
    const = lambda i: (0, 0)
    rows3 = pl.BlockSpec((tm, N_HEADS, D_HEAD), lambda i: (i, 0, 0))
    return pl.pallas_call(
        _inproj_kernel,
        grid=(m // tm,),
        in_specs=[pl.BlockSpec((tm, D_MODEL), row),
                  _resident((1, D_MODEL), const),
                  _resident((D_MODEL, N_PROJ), const)],
        out_specs=[pl.BlockSpec((tm, N_MAIN), row),
                   pl.BlockSpec((tm, N_GATE), row),
                   pl.BlockSpec((tm, 3 * W_BRANCH), row),
                   rows3, rows3],
        out_shape=[jax.ShapeDtypeStruct((m, N_MAIN), F32),
                   jax.ShapeDtypeStruct((m, N_GATE), BF16),
                   jax.ShapeDtypeStruct((m, 3 * W_BRANCH), BF16),
                   jax.ShapeDtypeStruct((m, N_HEADS, D_HEAD), F32),
                   jax.ShapeDtypeStruct((m, N_HEADS, D_HEAD), F32)],
        compiler_params=_params(),
        name="inproj",
    )(x, g, w)


def _bucket_of(rel):
    n = jnp.maximum(rel, 0)
    max_exact = NUM_BUCKETS // 2
    nf = jnp.maximum(n, max_exact).astype(F32)
    large = max_exact + (jnp.log(nf / max_exact) / math.log(MAX_DISTANCE / max_exact)
                         * (NUM_BUCKETS - max_exact)).astype(jnp.int32)
    large = jnp.minimum(large, NUM_BUCKETS - 1)
    return jnp.where(n < max_exact, n, large)


def _bias_of(bucket, rb_ref, head):
    val = jnp.zeros(bucket.shape, F32)
    for b in range(NUM_BUCKETS):
        val = jnp.where(bucket == b, rb_ref[b, head], val)
    return val


def _bias_kernel(rb_ref, band_ref, dec_ref, *, past, group_width):
    kk = lax.broadcasted_iota(jnp.int32, (LANES, LANES), 0)
    qq = lax.broadcasted_iota(jnp.int32, (LANES, LANES), 1)
    for d in range(2):
        rel = qq - kk + d * LANES
        bucket = _bucket_of(rel)
        for h in range(N_HEADS):
            val = (_bias_of(bucket, rb_ref, h) - rb_ref[NUM_BUCKETS - 1, h]) * LOG2E
            band_ref[d, h] = jnp.where(rel >= 0, val, NEG)
    rows, width = dec_ref.shape
    row = lax.broadcasted_iota(jnp.int32, (rows, width), 0)
    col = lax.broadcasted_iota(jnp.int32, (rows, width), 1)
    rel = past - ((row >> 3) * group_width + (col >> 2))
    bucket = _bucket_of(rel)
    head = col & (N_HEADS - 1)
    own = head == (row & (N_HEADS - 1))
    val = jnp.zeros((rows, width), F32)
    for h in range(N_HEADS):
        val = jnp.where(head == h, _bias_of(bucket, rb_ref, h), val)
    dec_ref[...] = jnp.where(own, val, NEG)


def _bias_tables(rel_bias, past, group_width):
    n_groups = past // group_width
    return pl.pallas_call(
        functools.partial(_bias_kernel, past=past, group_width=group_width),
        in_specs=[pl.BlockSpec(memory_space=pltpu.SMEM)],
        out_shape=[jax.ShapeDtypeStruct((2, N_HEADS, LANES, LANES), F32),
                   jax.ShapeDtypeStruct((n_groups * 8, group_width * N_HEADS), F32)],
        compiler_params=pltpu.CompilerParams(vmem_limit_bytes=VMEM_LIMIT_BYTES),
        name="bias_tables",
    )(rel_bias)


def _lambda(lq_ref, lam_init):
    lq = lq_ref[...]
    a = jnp.sum(lq[0:1] * lq[1:2], axis=-1, keepdims=True)
    b = jnp.sum(lq[2:3] * lq[3:4], axis=-1, keepdims=True)
    return jnp.exp(a) - jnp.exp(b) + lam_init


def _first_map_lanes(rows):
    lane = lax.broadcasted_iota(jnp.int32, (rows, D_HEAD), 1)
    return lane < DH_A


SOFTMAX_STRIP = 2 * LANES


def _softmax_update(h, s, v_h, m_sc, l_sc, acc_sc):
    m_prev = m_sc[h]
    m_new = jnp.maximum(m_prev, jnp.max(s, axis=0, keepdims=True))
    alpha = jnp.exp2(m_prev - m_new)
    p = jnp.exp2(s - m_new)
    l_sc[h] = alpha * l_sc[h] + jnp.sum(p, axis=0, keepdims=True)
    acc_sc[h] = alpha * acc_sc[h] + _dot_tn(v_h, p.astype(BF16))
    m_sc[h] = m_new


def _attn_kernel(q_ref, k_ref, v_ref, band_ref, lq_ref, na_ref, o_ref,
                 qbd_sc, m_sc, l_sc, acc_sc, s_sc, p_sc, a_sc, *, t, lam_init):
    i = pl.program_id(0)
    nsub = t // LANES
    first_map = _first_map_lanes(t)

    for h in range(N_HEADS):
        q_h = q_ref[:, h * D_HEAD:(h + 1) * D_HEAD]
        zero = jnp.zeros_like(q_h)
        qbd_sc[h, 0:t, :] = jnp.where(first_map, q_h, zero)
        qbd_sc[h, t:2 * t, :] = jnp.where(first_map, zero, q_h)
    m_sc[...] = jnp.full(m_sc.shape, NEG, F32)
    l_sc[...] = jnp.zeros(l_sc.shape, F32)
    acc_sc[...] = jnp.zeros(acc_sc.shape, F32)

    def scores(h, ks):
        k_h = k_ref[pl.ds(ks, t), h * D_HEAD:(h + 1) * D_HEAD]
        v_h = v_ref[pl.ds(ks, t), h * D_HEAD:(h + 1) * D_HEAD]
        return _dot_nt(k_h, qbd_sc[h]), v_h

    def add_band(s, h, lead_of):
        rows = []
        for kc in range(nsub):
            cols = []
            for mp in range(2):
                for qr in range(nsub):
                    sub = s[kc * LANES:(kc + 1) * LANES, mp * t + qr * LANES: mp * t + (qr + 1) * LANES]
                    lead = lead_of(kc, qr)
                    if lead < 0:
                        cols.append(jnp.full_like(sub, NEG))
                    elif lead < 2:
                        cols.append(sub + band_ref[lead, h])
                    else:
                        cols.append(sub)
            rows.append(jnp.concatenate(cols, axis=1))
        return jnp.concatenate(rows, axis=0) if nsub > 1 else rows[0]

    def stage_scores(h, ks, slot):
        s_sc[slot] = _dot_nt(k_ref[pl.ds(ks, t), h * D_HEAD:(h + 1) * D_HEAD], qbd_sc[h])

    def stage_softmax(h, slot):
        m_prev = m_sc[h]
        m_new = jnp.maximum(m_prev, jnp.max(s_sc[slot], axis=0, keepdims=True))
        alpha = jnp.exp2(m_prev - m_new)
        sums = []
        for c0 in range(0, 2 * t, SOFTMAX_STRIP):
            cs = slice(c0, c0 + SOFTMAX_STRIP)
            p = jnp.exp2(s_sc[slot, :, cs] - m_new[:, cs])
            sums.append(jnp.sum(p, axis=0, keepdims=True))
            p_sc[slot, :, cs] = p.astype(BF16)
        l_sc[h] = alpha * l_sc[h] + jnp.concatenate(sums, axis=1)
        a_sc[h] = alpha
        m_sc[h] = m_new

    def stage_values(h, ks, slot):
        v_h = v_ref[pl.ds(ks, t), h * D_HEAD:(h + 1) * D_HEAD]
        acc_sc[h] = a_sc[h] * acc_sc[h] + _dot_tn(v_h, p_sc[slot])

    def far_block(j, carry):
        ks = pl.multiple_of(j * t, t)
        stage_scores(0, ks, 0)
        for h in range(N_HEADS):
            if h + 1 < N_HEADS:
                stage_scores(h + 1, ks, (h + 1) % 2)
            stage_softmax(h, h % 2)
            stage_values(h, ks, h % 2)
        return carry

    lax.fori_loop(0, jnp.maximum(i - 1, 0), far_block, 0)

    @pl.when(i >= 1)
    def _():
        ks = pl.multiple_of((i - 1) * t, t)
        for h in range(N_HEADS):
            s, v_h = scores(h, ks)
            s = add_band(s, h, lambda kc, qr: qr - kc + nsub)
            _softmax_update(h, s, v_h, m_sc, l_sc, acc_sc)

    ks = pl.multiple_of(i * t, t)
    lam = _lambda(lq_ref, lam_init)
    for h in range(N_HEADS):
        s, v_h = scores(h, ks)
        s = add_band(s, h, lambda kc, qr: qr - kc)
        _softmax_update(h, s, v_h, m_sc, l_sc, acc_sc)
        o = acc_sc[h] / l_sc[h]
        attn = o[:, 0:t] - lam * o[:, t:2 * t]
        norm = lax.rsqrt(jnp.mean(attn * attn, axis=0, keepdims=True) + EPS)
        o_ref[h * D_HEAD:(h + 1) * D_HEAD, :] = (
            attn * norm * na_ref[...] * (1.0 - lam_init)).astype(o_ref.dtype)


def _prompt_attention(qkv, band, lambda_qk, norm_a_col, lam_init):
    seq = qkv.shape[0]
    t = min(512, seq)
    assert t % LANES == 0 and seq % t == 0 and t + 1 >= FAR_DISTANCE
    const2 = lambda i: (0, 0)
    return pl.pallas_call(
        functools.partial(_attn_kernel, t=t, lam_init=lam_init),
        grid=(seq // t,),
        in_specs=[pl.BlockSpec((t, W_BRANCH), lambda i: (i, 0)),
                  _resident((seq, W_BRANCH), lambda i: (0, 1)),
                  _resident((seq, W_BRANCH), lambda i: (0, 2)),
                  _resident((2, N_HEADS, LANES, LANES), lambda i: (0, 0, 0, 0)),
                  _resident((4, DH_A), const2),
                  _resident((D_HEAD, 1), const2)],
        out_specs=pl.BlockSpec((W_BRANCH, t), lambda i: (0, i)),
        out_shape=jax.ShapeDtypeStruct((W_BRANCH, seq), BF16),
        scratch_shapes=[pltpu.VMEM((N_HEADS, 2 * t, D_HEAD), BF16),
                        pltpu.VMEM((N_HEADS, 1, 2 * t), F32),
                        pltpu.VMEM((N_HEADS, 1, 2 * t), F32),
                        pltpu.VMEM((N_HEADS, D_HEAD, 2 * t), F32),
                        pltpu.VMEM((2, t, 2 * t), F32),
                        pltpu.VMEM((2, t, 2 * t), BF16),
                        pltpu.VMEM((N_HEADS, 1, 2 * t), F32)],
        compiler_params=_params(),
        name="prompt_attention",
    )(qkv, qkv, qkv, band, lambda_qk, norm_a_col)


DECODE_PAGES_PER_STEP = 16


def _rows_by_head(x_row):
    pieces = [x_row[:, h * D_HEAD:(h + 1) * D_HEAD] for h in range(N_HEADS)]
    return jnp.concatenate(pieces + pieces, axis=0)


def _decode_kernel(pt_ref, *refs, n_pages_step, lam_init):
    del pt_ref
    k_refs = refs[:n_pages_step]
    v_refs = refs[n_pages_step:2 * n_pages_step]
    (dec_ref, rb_ref, lq_ref, na_ref, q_ref, ks_ref, vs_ref, o_ref,
     qrows_sc, m_sc, l_sc, acc_sc) = refs[2 * n_pages_step:]
    g = pl.program_id(1)
    n_rows = 2 * N_HEADS
    row = lax.broadcasted_iota(jnp.int32, (n_rows, D_HEAD), 0)
    lane = lax.broadcasted_iota(jnp.int32, (n_rows, D_HEAD), 1)
    own_map = (lane >> 6) == (row >> 2)

    @pl.when(g == 0)
    def _():
        q8 = _rows_by_head(q_ref[...] * (DH_A ** -0.5))
        qrows_sc[...] = jnp.where(own_map, q8, 0.0)
        m_sc[...] = jnp.full(m_sc.shape, NEG, F32)
        l_sc[...] = jnp.zeros(l_sc.shape, F32)
        acc_sc[...] = jnp.zeros(acc_sc.shape, F32)

    qrows = qrows_sc[...]
    qb = qrows.astype(BF16)
    s = jnp.concatenate([_dot_nt(qb, k_refs[r][...].astype(BF16)) for r in range(n_pages_step)], axis=1)
    s = s + dec_ref[...]
    m_prev = m_sc[...]
    m_new = jnp.maximum(m_prev, jnp.max(s, axis=-1, keepdims=True))
    alpha = jnp.exp(m_prev - m_new)
    p = jnp.exp(s - m_new)
    l_sc[...] = alpha * l_sc[...] + jnp.sum(p, axis=-1, keepdims=True)
    pb = p.astype(BF16)
    pv = None
    for r in range(n_pages_step):
        part = _dot(pb[:, r * PAGE_ROWS:(r + 1) * PAGE_ROWS], v_refs[r][...].astype(BF16))
        pv = part if pv is None else pv + part
    acc_sc[...] = alpha * acc_sc[...] + pv
    m_sc[...] = m_new

    @pl.when(g == pl.num_programs(1) - 1)
    def _():
        head = lax.broadcasted_iota(jnp.int32, (n_rows, 1), 0) & (N_HEADS - 1)
        b0 = jnp.zeros((n_rows, 1), F32)
        for h in range(N_HEADS):
            b0 = jnp.where(head == h, rb_ref[0, h], b0)
        s_own = jnp.sum(qrows * _rows_by_head(ks_ref[...]), axis=-1, keepdims=True) + b0
        m_prev = m_sc[...]
        m_fin = jnp.maximum(m_prev, s_own)
        alpha = jnp.exp(m_prev - m_fin)
        p_own = jnp.exp(s_own - m_fin)
        l_fin = alpha * l_sc[...] + p_own
        acc = alpha * acc_sc[...] + p_own * _rows_by_head(vs_ref[...])
        o = acc / l_fin
        lam = _lambda(lq_ref, lam_init)
        attn = o[0:N_HEADS] - lam * o[N_HEADS:n_rows]
        out = _rms(attn, na_ref[...]) * (1.0 - lam_init)
        for h in range(N_HEADS):
            o_ref[:, h * D_HEAD:(h + 1) * D_HEAD] = out[h:h + 1].astype(o_ref.dtype)


def _sample_attention(p_s, cache_k, cache_v, layer, page_table, dec_bias, rel_bias, lambda_qk, norm_a,
                      lam_init, n_pages_step):
    db = p_s.shape[0]
    n_pages = page_table.shape[1]
    n_groups = n_pages // n_pages_step
    gw = n_pages_step * PAGE_ROWS
    p3 = p_s.reshape(db, 1, N_MAIN)
    dec3 = dec_bias.reshape(n_groups, 8, gw)

    def page_spec(r):
        return pl.BlockSpec((None, None, PAGE_ROWS, D_HEAD),
                            lambda b, g, pt: (layer, pt[b, g * n_pages_step + r], 0, 0))

    def row_spec(col):
        return pl.BlockSpec((None, 1, W_BRANCH), lambda b, g, pt: (b, 0, col // W_BRANCH))

    const2 = lambda b, g, pt: (0, 0)
    grid_spec = pltpu.PrefetchScalarGridSpec(
        num_scalar_prefetch=1,
        grid=(db, n_groups),
        in_specs=([page_spec(r) for r in range(n_pages_step)]
                  + [page_spec(r) for r in range(n_pages_step)]
                  + [pl.BlockSpec((None, 8, gw), lambda b, g, pt: (g, 0, 0)),
                     pl.BlockSpec(memory_space=pltpu.SMEM),
                     pl.BlockSpec((4, DH_A), const2),
                     pl.BlockSpec((1, D_HEAD), const2),
                     row_spec(C_AQ), row_spec(C_AK), row_spec(C_AV)]),
        out_specs=pl.BlockSpec((None, 1, W_BRANCH), lambda b, g, pt: (b, 0, 0)),
        scratch_shapes=[pltpu.VMEM((8, D_HEAD), F32),
                        pltpu.VMEM((8, 1), F32),
                        pltpu.VMEM((8, 1), F32),
                        pltpu.VMEM((8, D_HEAD), F32)])
    out = pl.pallas_call(
        functools.partial(_decode_kernel, n_pages_step=n_pages_step, lam_init=lam_init),
        grid_spec=grid_spec,
        out_shape=jax.ShapeDtypeStruct((db, 1, W_BRANCH), BF16),
        compiler_params=_params(2),
        name="sample_attention",
    )(page_table, *([cache_k] * n_pages_step), *([cache_v] * n_pages_step),
      dec3, rel_bias, lambda_qk, norm_a, p3, p3, p3)
    return out.reshape(db, W_BRANCH)


def _log_sigmoid(x):
    return jnp.minimum(x, 0.0) - jnp.log1p(jnp.exp(-jnp.abs(x)))


def _split_bf16(x):
    hi = x.astype(BF16)
    lo = (x - hi.astype(F32)).astype(BF16)
    return hi, lo


def _dot_3pass(a, b):
    a_hi, a_lo = _split_bf16(a)
    b_hi, b_lo = _split_bf16(b)
    return _dot(a_hi, b_hi) + (_dot(a_hi, b_lo) + _dot(a_lo, b_hi))


def _gla_log_decay(bg, wup_ref, bgla_ref):
    x = _dot_3pass(bg, wup_ref[...])
    return _log_sigmoid(x + bgla_ref[...]) / GLA_TAU


def _rotate_half(x):
    width = x.shape[-1]
    lane = lax.broadcasted_iota(jnp.int32, x.shape, x.ndim - 1)
    lo = (lane & (DK - 1)) < DK // 2
    return jnp.where(lo, pltpu.roll(x, width - DK // 2, x.ndim - 1), pltpu.roll(x, DK // 2, x.ndim - 1))


def _rotary(x, cos, sin_signed):
    return x * cos + _rotate_half(x) * sin_signed


def _log_gamma(h):
    return math.log(1.0 - 2.0 ** (-5.0 - h))


def _head_of_keys(shape, dim):
    return lax.broadcasted_iota(jnp.int32, shape, dim) >> 6


def _layernorm(x, g):
    xc = x - jnp.mean(x, axis=-1, keepdims=True)
    return xc * lax.rsqrt(jnp.mean(xc * xc, axis=-1, keepdims=True) + EPS) * g


def _block_diag_mask():
    r = lax.broadcasted_iota(jnp.int32, (W_KEYS, W_BRANCH), 0) >> 6
    c = lax.broadcasted_iota(jnp.int32, (W_KEYS, W_BRANCH), 1) >> 7
    return r == c


def _head_blocks(full):
    return jnp.concatenate([full[h * DK:(h + 1) * DK, h * D_HEAD:(h + 1) * D_HEAD]
                            for h in range(N_HEADS)], axis=0)


GROUP = 256


def _linattn_kernel(bq_ref, bk_ref, bv_ref, bg_ref, cq_ref, ck_ref, cv_ref, cos_ref, sin_ref,
                    wup_ref, bgla_ref, nb_ref, nc_ref,
                    bo_ref, co_ref, sg_ref, sr_ref,
                    stg_sc, str_sc, *, rows):
    step = pl.program_id(0)

    @pl.when(step == 0)
    def _():
        stg_sc[...] = jnp.zeros(stg_sc.shape, F32)
        str_sc[...] = jnp.zeros(str_sc.shape, F32)

    n_chunks = rows // CHUNK
    n_groups = rows // GROUP
    cpg = GROUP // CHUNK
    ti = lax.broadcasted_iota(jnp.int32, (GROUP, GROUP), 0)
    si = lax.broadcasted_iota(jnp.int32, (GROUP, GROUP), 1)
    causal = ((ti >> 6) == (si >> 6)) & (ti >= si)
    tril = jnp.where(causal, 1.0, 0.0).astype(BF16)
    dist = (ti - si).astype(F32)
    key_head = _head_of_keys((GROUP, W_KEYS), 1)

    tcol = (lax.broadcasted_iota(jnp.int32, (GROUP, W_KEYS), 0) & (CHUNK - 1)).astype(F32)
    lg_lane = jnp.zeros((GROUP, W_KEYS), F32)
    lg_row = jnp.zeros((W_KEYS, D_HEAD), F32)
    row_head = _head_of_keys((W_KEYS, D_HEAD), 0)
    for h in range(N_HEADS):
        lg_lane = jnp.where(key_head == h, _log_gamma(h), lg_lane)
        lg_row = jnp.where(row_head == h, _log_gamma(h), lg_row)
    g_fwd = jnp.exp((tcol + 1.0) * lg_lane)
    g_rev = jnp.exp((CHUNK - 1.0 - tcol) * lg_lane)
    g_end = jnp.exp(CHUNK * lg_row)

    glog = _gla_log_decay(bg_ref[...], wup_ref, bgla_ref)
    cq_all = _rotary(cq_ref[...], cos_ref[...], sin_ref[...])
    ck_all = _rotary(ck_ref[...], cos_ref[...], sin_ref[...]) * (DK ** -0.5)
    bq_all = bq_ref[...] * (DK ** -0.5)

    g_qd, g_oi, g_upd, g_end_rows = [], [], [], []
    r_qf, r_oi, r_upd = [], [], []
    for g in range(n_groups):
        rs = slice(g * GROUP, (g + 1) * GROUP)
        g_hi, g_lo = _split_bf16(glog[rs])
        gcum = _dot(tril, g_hi) + _dot(tril, g_lo)
        ends = [gcum[(c + 1) * CHUNK - 1:(c + 1) * CHUNK] for c in range(cpg)]
        g_end_rows += ends
        gend_b = jnp.concatenate([jnp.broadcast_to(e, (CHUNK, W_KEYS)) for e in ends], axis=0)
        qd = bq_all[rs] * jnp.exp(gcum)
        bk = bk_ref[rs, :]
        kinv = (bk * jnp.exp(-gcum)).astype(BF16)
        krem = (bk * jnp.exp(gend_b - gcum)).astype(BF16)
        v = bv_ref[rs, :].astype(BF16)
        parts = []
        for h in range(N_HEADS):
            qm = jnp.where(key_head == h, qd, 0.0).astype(BF16)
            sc = jnp.where(causal, _dot_nt(qm, kinv), 0.0)
            parts.append(_dot(sc.astype(BF16), v[:, h * D_HEAD:(h + 1) * D_HEAD]))
        oi = jnp.concatenate(parts, axis=1)
        qdb = qd.astype(BF16)
        for c in range(cpg):
            cs = slice(c * CHUNK, (c + 1) * CHUNK)
            g_qd.append(qdb[cs])
            g_oi.append(oi[cs])
            g_upd.append(_head_blocks(_dot_tn(krem[cs], v[cs])))
        q = cq_all[rs]
        k = ck_all[rs]
        v = cv_ref[rs, :].astype(BF16)
        kb = k.astype(BF16)
        parts = []
        for h in range(N_HEADS):
            qm = jnp.where(key_head == h, q, 0.0).astype(BF16)
            decay = jnp.where(causal, jnp.exp(dist * _log_gamma(h)), 0.0)
            sc = _dot_nt(qm, kb) * decay
            parts.append(_dot(sc.astype(BF16), v[:, h * D_HEAD:(h + 1) * D_HEAD]))
        oi = jnp.concatenate(parts, axis=1)
        krev = (k * g_rev).astype(BF16)
        qfb = (q * g_fwd).astype(BF16)
        for c in range(cpg):
            cs = slice(c * CHUNK, (c + 1) * CHUNK)
            r_qf.append(qfb[cs])
            r_oi.append(oi[cs])
            r_upd.append(_head_blocks(_dot_tn(krev[cs], v[cs])))

    e_cols = jnp.exp(jnp.transpose(jnp.concatenate(g_end_rows, axis=0)))

    bd = _block_diag_mask()

    def state_products(qs, st):
        stb = st.astype(BF16)
        st_bd = jnp.where(bd, jnp.concatenate([stb] * N_HEADS, axis=1), jnp.zeros((), BF16))
        return _dot(qs, st_bd)

    st = stg_sc[...]
    for c in range(n_chunks):
        rs = slice(c * CHUNK, (c + 1) * CHUNK)
        o = g_oi[c] + state_products(g_qd[c], st)
        st = e_cols[:, c:c + 1] * st + g_upd[c]
        for h in range(N_HEADS):
            hs = slice(h * D_HEAD, (h + 1) * D_HEAD)
            bo_ref[rs, hs] = _rms(o[:, hs], nb_ref[:, hs]).astype(bo_ref.dtype)
    stg_sc[...] = st
    st = str_sc[...]
    for c in range(n_chunks):
        rs = slice(c * CHUNK, (c + 1) * CHUNK)
        o = r_oi[c] + state_products(r_qf[c], st)
        st = g_end * st + r_upd[c]
        for h in range(N_HEADS):
            hs = slice(h * D_HEAD, (h + 1) * D_HEAD)
            co_ref[rs, hs] = _layernorm(o[:, hs], nc_ref[:, hs]).astype(co_ref.dtype)
    str_sc[...] = st

    @pl.when(step == pl.num_programs(0) - 1)
    def _():
        for h in range(N_HEADS):
            sg_ref[h] = stg_sc[h * DK:(h + 1) * DK, :]
            sr_ref[h] = str_sc[h * DK:(h + 1) * DK, :]


def _prompt_linear_attention(p, cos, sin, wup, bgla, norm_b, norm_c):
    seq = p.shape[0]
    rows = min(512, seq)

    def col(c, w):
        return pl.BlockSpec((rows, w), lambda i: (i, c // w))

    const = lambda i: (0, 0)
    state = jax.ShapeDtypeStruct((N_HEADS, DK, D_HEAD), F32)
    state_spec = pl.BlockSpec((N_HEADS, DK, D_HEAD), lambda i: (0, 0, 0))
    return pl.pallas_call(
        functools.partial(_linattn_kernel, rows=rows),
        grid=(seq // rows,),
        in_specs=[col(C_BQ, W_KEYS), col(C_BK, W_KEYS), col(C_BV, W_BRANCH), col(C_BG, LANES),
                  col(C_CQ, W_KEYS), col(C_CK, W_KEYS), col(C_CV, W_BRANCH),
                  pl.BlockSpec((rows, W_KEYS), lambda i: (i, 0)),
                  pl.BlockSpec((rows, W_KEYS), lambda i: (i, 0)),
                  pl.BlockSpec((LANES, W_KEYS), const),
                  pl.BlockSpec((1, W_KEYS), const),
                  pl.BlockSpec((1, W_BRANCH), const),
                  pl.BlockSpec((1, W_BRANCH), const)],
        out_specs=[pl.BlockSpec((rows, W_BRANCH), lambda i: (i, 0)),
                   pl.BlockSpec((rows, W_BRANCH), lambda i: (i, 0)),
                   state_spec, state_spec],
        out_shape=[jax.ShapeDtypeStruct((seq, W_BRANCH), BF16),
                   jax.ShapeDtypeStruct((seq, W_BRANCH), BF16),
                   state, state],
        scratch_shapes=[pltpu.VMEM((W_KEYS, D_HEAD), F32),
                        pltpu.VMEM((W_KEYS, D_HEAD), F32)],
        compiler_params=_params(),
        name="prompt_linear_attention",
    )(p, p, p, p, p, p, p, cos, sin, wup, bgla, norm_b, norm_c)


def _to_column(x_row):
    n = x_row.shape[-1]
    r = lax.broadcasted_iota(jnp.int32, (n, n), 0)
    c = lax.broadcasted_iota(jnp.int32, (n, n), 1)
    return jnp.sum(jnp.where(r == c, jnp.broadcast_to(x_row, (n, n)), 0.0), axis=-1, keepdims=True)


def _recurrent_kernel(bq_ref, bk_ref, bv_ref, bg_ref, cq_ref, ck_ref, cv_ref, cos_ref, sin_ref,
                      wup_ref, bgla_ref, nb_ref, nc_ref, sg0_ref, sr0_ref,
                      bo_ref, co_ref, sg_ref, sr_ref):
    decay = _to_column(jnp.exp(_gla_log_decay(bg_ref[...], wup_ref, bgla_ref)))
    bq = _to_column(bq_ref[...] * (DK ** -0.5))
    bk = _to_column(bk_ref[...])
    cq = _to_column(_rotary(cq_ref[...], cos_ref[...], sin_ref[...]))
    ck = _to_column(_rotary(ck_ref[...], cos_ref[...], sin_ref[...]) * (DK ** -0.5))
    for h in range(N_HEADS):
        ks = slice(h * DK, (h + 1) * DK)
        hs = slice(h * D_HEAD, (h + 1) * D_HEAD)
        s_new = decay[ks] * sg0_ref[h] + bk[ks] * bv_ref[:, hs]
        sg_ref[h] = s_new
        o = jnp.sum(bq[ks] * s_new, axis=0, keepdims=True)
        bo_ref[:, hs] = _rms(o, nb_ref[:, hs]).astype(bo_ref.dtype)
        s_new = math.exp(_log_gamma(h)) * sr0_ref[h] + ck[ks] * cv_ref[:, hs]
        sr_ref[h] = s_new
        o = jnp.sum(cq[ks] * s_new, axis=0, keepdims=True)
        co_ref[:, hs] = _layernorm(o, nc_ref[:, hs]).astype(co_ref.dtype)


def _sample_linear_attention(p_s, cos, sin, wup, bgla, norm_b, norm_c, state_gla, state_ret, layer):
    db = p_s.shape[0]
    p3 = p_s.reshape(db, 1, N_MAIN)

    def col(c, w):
        return pl.BlockSpec((None, 1, w), lambda b: (b, 0, c // w))

    const = lambda b: (0, 0)
    state_spec = pl.BlockSpec((None, N_HEADS, DK, D_HEAD), lambda b: (b, 0, 0, 0))
    state_in_spec = pl.BlockSpec((None, None, N_HEADS, DK, D_HEAD), lambda b: (layer, b, 0, 0, 0))
    out_row = pl.BlockSpec((None, 1, W_BRANCH), lambda b: (b, 0, 0))
    state = jax.ShapeDtypeStruct((db, N_HEADS, DK, D_HEAD), F32)
    bo, co, sg, sr = pl.pallas_call(
        _recurrent_kernel,
        grid=(db,),
        in_specs=[col(C_BQ, W_KEYS), col(C_BK, W_KEYS), col(C_BV, W_BRANCH), col(C_BG, LANES),
                  col(C_CQ, W_KEYS), col(C_CK, W_KEYS), col(C_CV, W_BRANCH),
                  pl.BlockSpec((1, W_KEYS), const), pl.BlockSpec((1, W_KEYS), const),
                  pl.BlockSpec((LANES, W_KEYS), const),
                  pl.BlockSpec((1, W_KEYS), const),
                  pl.BlockSpec((1, W_BRANCH), const),
                  pl.BlockSpec((1, W_BRANCH), const),
                  state_in_spec, state_in_spec],
        out_specs=[out_row, out_row, state_spec, state_spec],
        out_shape=[jax.ShapeDtypeStruct((db, 1, W_BRANCH), BF16),
                   jax.ShapeDtypeStruct((db, 1, W_BRANCH), BF16),
                   state, state],
        compiler_params=_params(),
        name="sample_linear_attention",
    )(p3, p3, p3, p3, p3, p3, p3, cos, sin, wup, bgla, norm_b, norm_c, state_gla, state_ret)
    return bo.reshape(db, W_BRANCH), co.reshape(db, W_BRANCH), sg, sr


def _merge_kernel(x_ref, aot_ref, bo_ref, co_ref, br_ref, cg_ref, gates_ref,
                  wa_ref, wb_ref, wc_ref, wo_ref, o_ref):
    ya = _dot_tn(aot_ref[...], wa_ref[...])
    br = br_ref[...].astype(F32)
    yb = _dot((bo_ref[...].astype(F32) * (br * _sigmoid(br))).astype(BF16), wb_ref[...])
    cg = cg_ref[...].astype(F32)
    yc = _dot((co_ref[...].astype(F32) * (cg * _sigmoid(cg))).astype(BF16), wc_ref[...])
    mix = (_sigmoid(gates_ref[:, 0:D_MODEL].astype(F32)) * ya
           + _sigmoid(gates_ref[:, D_MODEL:2 * D_MODEL].astype(F32)) * yb
           + _sigmoid(gates_ref[:, 2 * D_MODEL:3 * D_MODEL].astype(F32)) * yc)
    o_ref[...] = x_ref[...] + _dot(mix.astype(BF16), wo_ref[...])


def _merge(x, aot, bo, co, p, wa, wb, wc, wo):
    m = x.shape[0]
    tm = min(512, m)
    row = lambda i: (i, 0)
    const = lambda i: (0, 0)
    return pl.pallas_call(
        _merge_kernel,
        grid=(m // tm,),
        in_specs=[pl.BlockSpec((tm, D_MODEL), row),
                  pl.BlockSpec((W_BRANCH, tm), lambda i: (0, i)),
                  pl.BlockSpec((tm, W_BRANCH), row),
                  pl.BlockSpec((tm, W_BRANCH), row),
                  pl.BlockSpec((tm, W_BRANCH), lambda i: (i, C_BR // W_BRANCH)),
                  pl.BlockSpec((tm, W_BRANCH), lambda i: (i, C_CG // W_BRANCH)),
                  pl.BlockSpec((tm, 3 * D_MODEL), lambda i: (i, C_GATES // (3 * D_MODEL))),
                  _resident((W_BRANCH, D_MODEL), const),
                  _resident((W_BRANCH, D_MODEL), const),
                  _resident((W_BRANCH, D_MODEL), const),
                  _resident((D_MODEL, D_MODEL), const)],
        out_specs=pl.BlockSpec((tm, D_MODEL), row),
        out_shape=jax.ShapeDtypeStruct((m, D_MODEL), F32),
        compiler_params=_params(),
        name="merge",
    )(x, aot, bo, co, p, p, p, wa, wb, wc, wo)


def _rope_tables(pos):
    half = DK // 2
    inv = ROPE_BASE ** (-jnp.arange(half, dtype=F32) / half)
    ang = pos.astype(F32)[:, None] * inv[None, :]
    cos = jnp.cos(ang)
    sin = jnp.sin(ang)
    cos_t = jnp.tile(jnp.concatenate([cos, cos], axis=-1), (1, N_HEADS))
    sin_t = jnp.tile(jnp.concatenate([-sin, sin], axis=-1), (1, N_HEADS))
    return cos_t, sin_t


def _reorder_w_in(w):
    s_a = 3 * W_BRANCH
    s_b = s_a + 2 * W_KEYS + 2 * W_BRANCH
    s_g = s_b + GLA_RANK
    s_c = s_g + 2 * W_KEYS + 2 * W_BRANCH
    s_br = s_b - W_BRANCH
    s_cg = s_c - W_BRANCH
    pad = jnp.zeros((w.shape[0], N_MAIN - C_BG - GLA_RANK), w.dtype)
    return jnp.concatenate([w[:, s_c:], w[:, s_br:s_b], w[:, s_cg:s_c],
                            w[:, :s_br], w[:, s_g:s_cg], w[:, s_b:s_g], pad], axis=1)


def kernel(x_prompt, x_sample, cache_k, cache_v, state_gla, state_ret, page_table, w_in, w_gla_up, b_gla, lambda_qk, rel_bias, norm_a, norm_b, norm_c, w_a, w_b, w_c, w_out, norm_ffn1, ffn1_wg, ffn1_wu, ffn1_wd, norm_mix, norm_ffn2, ffn2_wg, ffn2_wu, ffn2_wd, norm_final):
    batch, seq, _ = x_prompt.shape
    db, dec_seq, _ = x_sample.shape
    depth = w_in.shape[0]
    assert batch == 1 and dec_seq == 1
    n_pages = page_table.shape[1]
    past = n_pages * PAGE_SIZE
    n_pool = cache_k.shape[1]
    n_pages_step = math.gcd(DECODE_PAGES_PER_STEP, n_pages)

    band, dec_bias = _bias_tables(rel_bias, past, n_pages_step * PAGE_SIZE)
    cos_p, sin_p = _rope_tables(jnp.arange(seq))
    cos_s, sin_s = _rope_tables(past + jnp.arange(dec_seq))
    ck4 = cache_k.reshape(depth, n_pool, PAGE_ROWS, D_HEAD)
    cv4 = cache_v.reshape(depth, n_pool, PAGE_ROWS, D_HEAD)
    g_final = norm_final.reshape(1, D_MODEL)

    xp = x_prompt.reshape(seq, D_MODEL)
    xs = x_sample.reshape(db, D_MODEL)
    outs = {k: [] for k in ("kp", "vp", "gp", "rp", "ks", "vs", "gs", "rs")}
    for l in range(depth):
        lam_init = 0.8 - 0.6 * math.exp(-0.3 * l)
        last = l == depth - 1
        w_in_l = _reorder_w_in(w_in[l]).astype(BF16)
        wup = jnp.concatenate([w_gla_up[l], jnp.zeros((LANES - GLA_RANK, W_KEYS), F32)], axis=0)
        bgla = b_gla[l].reshape(1, W_KEYS)
        nb = norm_b[l].reshape(1, W_BRANCH)
        nc = norm_c[l].reshape(1, W_BRANCH)
        na = norm_a[l].reshape(1, D_HEAD)
        na_col = norm_a[l].reshape(D_HEAD, 1)
        ffn1 = (norm_ffn1[l].reshape(1, D_MODEL), ffn1_wg[l].astype(BF16), ffn1_wu[l].astype(BF16),
                ffn1_wd[l].astype(BF16))
        ffn2 = (norm_ffn2[l].reshape(1, D_MODEL), ffn2_wg[l].astype(BF16), ffn2_wu[l].astype(BF16),
                ffn2_wd[l].astype(BF16))
        g_mix = norm_mix[l].reshape(1, D_MODEL)
        proj_w = (w_a[l].astype(BF16), w_b[l].astype(BF16), w_c[l].astype(BF16), w_out[l].astype(BF16))

        xp = _ffn_half(xp, *ffn1, g_final, False)
        p, gate_cols, qkv, k_new, v_new = _inproj(xp, g_mix, w_in_l)
        aot = _prompt_attention(qkv, band, lambda_qk[l], na_col, lam_init)
        bo, co, sg, sr = _prompt_linear_attention(p, cos_p, sin_p, wup, bgla, nb, nc)
        xp = _merge(xp, aot, bo, co, gate_cols, *proj_w)
        xp = _ffn_half(xp, *ffn2, g_final, last)
        outs["kp"].append(k_new[None])
        outs["vp"].append(v_new[None])
        outs["gp"].append(sg[None])
        outs["rp"].append(sr[None])

        xs = _ffn_half(xs, *ffn1, g_final, False)
        p_s, gate_cols, _, k_new, v_new = _inproj(xs, g_mix, w_in_l)
        ao = _sample_attention(p_s, ck4, cv4, l, page_table, dec_bias, rel_bias, lambda_qk[l], na,
                               lam_init, n_pages_step)
        bo, co, sg, sr = _sample_linear_attention(p_s, cos_s, sin_s, wup, bgla, nb, nc,
                                                  state_gla, state_ret, l)
        xs = _merge(xs, ao.T, bo, co, gate_cols, *proj_w)
        xs = _ffn_half(xs, *ffn2, g_final, last)
        outs["ks"].append(k_new[:, None])
        outs["vs"].append(v_new[:, None])
        outs["gs"].append(sg)
        outs["rs"].append(sr)

    st = lambda key: jnp.stack(outs[key])
    return (xp.reshape(batch, seq, D_MODEL), xs.reshape(db, dec_seq, D_MODEL),
            st("kp"), st("vp"), st("gp"), st("rp"), st("ks"), st("vs"), st("gs"), st("rs"))
```

```python
import functools
import math

import jax
import jax.numpy as jnp
import numpy as np
from jax import lax
from jax.experimental import pallas as pl
from jax.experimental.pallas import tpu as pltpu

F32 = jnp.float32
BF16 = jnp.bfloat16

D_MODEL = 1024
N_HEADS = 4
DH_A = 64
D_HEAD = 128
DK = 64
GLA_RANK = 16
GLA_TAU = 16.0
ROPE_BASE = 10000.0
NUM_BUCKETS = 32
MAX_DISTANCE = 128
D_FF = 2816
PAGE_SIZE = 128
CHUNK = 64
EPS = 1e-6
NEG = -1e30
LOG2E = math.log2(math.e)
W_BRANCH = N_HEADS * D_HEAD
W_KEYS = N_HEADS * DK
PAGE_ROWS = PAGE_SIZE * N_HEADS

LANES = 128
SUBLANES = 8
VMEM_LIMIT_BYTES = 56 * 1024 * 1024

C_GATES = 0
C_BR = 3072
C_CG = 3584
N_GATE = 4096
C_AQ = 0
C_AK = 512
C_AV = 1024
C_BQ = 1536
C_BK = 1792
C_BV = 2048
C_CQ = 2560
C_CK = 2816
C_CV = 3072
C_BG = 3584
N_MAIN = 3712
N_PROJ = N_GATE + N_MAIN

FAR_DISTANCE = 113


def _params(n_axes=1, vmem=VMEM_LIMIT_BYTES):
    return pltpu.CompilerParams(dimension_semantics=("arbitrary",) * n_axes,
                                vmem_limit_bytes=vmem)


def _resident(shape, index_map):
    return pl.BlockSpec(shape, index_map, pipeline_mode=pl.Buffered(1))


def _rms(x, g):
    return x * lax.rsqrt(jnp.mean(x * x, axis=-1, keepdims=True) + EPS) * g


def _sigmoid(x):
    return 1.0 / (1.0 + jnp.exp(-x))


def _dot(a, b):
    return jnp.dot(a, b, preferred_element_type=F32)


def _dot_nt(a, b):
    return lax.dot_general(a, b, (((1,), (1,)), ((), ())), preferred_element_type=F32)


def _dot_tn(a, b, precision=None):
    return lax.dot_general(a, b, (((0,), (0,)), ((), ())), preferred_element_type=F32,
                           precision=precision)


FF_CHUNK = 1408


def _ffn_kernel(x_ref, g_ref, wg_ref, wu_ref, wd_ref, gf_ref, o_ref, *, final_norm):
    x = x_ref[...]
    n = _rms(x, g_ref[...]).astype(BF16)
    acc = None
    for c in range(0, D_FF, FF_CHUNK):
        a = _dot(n, wg_ref[:, c:c + FF_CHUNK])
        u = _dot(n, wu_ref[:, c:c + FF_CHUNK])
        h = (a * _sigmoid(a) * u).astype(BF16)
        part = _dot(h, wd_ref[c:c + FF_CHUNK, :])
        acc = part if acc is None else acc + part
    y = x + 0.5 * acc
    if final_norm:
        y = _rms(y, gf_ref[...])
    o_ref[...] = y


def _ffn_half(x, g, wg, wu, wd, layer, g_final, final_norm):
    m = x.shape[0]
    tm = min(512, m)
    row = lambda i: (i, 0)
    const = lambda i: (0, 0)
    of_layer = lambda i: (layer, 0, 0)
    return pl.pallas_call(
        functools.partial(_ffn_kernel, final_norm=final_norm),
        grid=(m // tm,),
        in_specs=[pl.BlockSpec((tm, D_MODEL), row),
                  _resident((1, D_MODEL), const),
                  _resident((None, D_MODEL, D_FF), of_layer),
                  _resident((None, D_MODEL, D_FF), of_layer),
                  _resident((None, D_FF, D_MODEL), of_layer),
                  _resident((1, D_MODEL), const)],
        out_specs=pl.BlockSpec((tm, D_MODEL), row),
        out_shape=jax.ShapeDtypeStruct((m, D_MODEL), F32),
        compiler_params=_params(),
        name="ffn_half",
    )(x, g, wg, wu, wd, g_final)


PROJ_CHUNK = 1024


def _inproj_kernel(x_ref, g_ref, w_ref, p_ref, gate_ref, qkv_ref, k_ref, v_ref):
    n = _rms(x_ref[...], g_ref[...]).astype(BF16)
    for c in range(0, N_GATE, PROJ_CHUNK):
        gate_ref[:, c:c + PROJ_CHUNK] = _dot(n, w_ref[:, c:c + PROJ_CHUNK]).astype(BF16)
    for c in range(0, N_MAIN, PROJ_CHUNK):
        w = min(PROJ_CHUNK, N_MAIN - c)
        p_ref[:, c:c + w] = _dot(n, w_ref[:, N_GATE + c:N_GATE + c + w])
    qkv_ref[:, 0:W_BRANCH] = (p_ref[:, C_AQ:C_AQ + W_BRANCH] * (DH_A ** -0.5 * LOG2E)).astype(BF16)
    qkv_ref[:, W_BRANCH:3 * W_BRANCH] = p_ref[:, C_AK:C_AK + 2 * W_BRANCH].astype(BF16)
    for h in range(N_HEADS):
        k_ref[:, h, :] = p_ref[:, C_AK + h * D_HEAD:C_AK + (h + 1) * D_HEAD]
        v_ref[:, h, :] = p_ref[:, C_AV + h * D_HEAD:C_AV + (h + 1) * D_HEAD]


def _inproj(x, g, w, layer):
    m = x.shape[0]
    tm = min(256, m)
    row = lambda i: (i, 0)
    const = lambda i: (0, 0)
    rows3 = pl.BlockSpec((tm, N_HEADS, D_HEAD), lambda i: (i, 0, 0))
    return pl.pallas_call(
        _inproj_kernel,
        grid=(m // tm,),
        in_specs=[pl.BlockSpec((tm, D_MODEL), row),
                  _resident((1, D_MODEL), const),
                  _resident((None, D_MODEL, N_PROJ), lambda i: (layer, 0, 0))],
        out_specs=[pl.BlockSpec((tm, N_MAIN), row),
                   pl.BlockSpec((tm, N_GATE), row),
                   pl.BlockSpec((tm, 3 * W_BRANCH), row),
                   rows3, rows3],
        out_shape=[jax.ShapeDtypeStruct((m, N_MAIN), F32),
                   jax.ShapeDtypeStruct((m, N_GATE), BF16),
                   jax.ShapeDtypeStruct((m, 3 * W_BRANCH), BF16),
                   jax.ShapeDtypeStruct((m, N_HEADS, D_HEAD), F32),
                   jax.ShapeDtypeStruct((m, N_HEADS, D_HEAD), F32)],
        compiler_params=_params(),
        name="inproj",
    )(x, g, w)


def _bucket_of(rel):
    n = jnp.maximum(rel, 0)
    max_exact = NUM_BUCKETS // 2
    nf = jnp.maximum(n, max_exact).astype(F32)
    large = max_exact + (jnp.log(nf / max_exact) / math.log(MAX_DISTANCE / max_exact)
                         * (NUM_BUCKETS - max_exact)).astype(jnp.int32)
    large = jnp.minimum(large, NUM_BUCKETS - 1)
    return jnp.where(n < max_exact, n, large)


def _bias_of(bucket, rb_ref, head):
    val = jnp.zeros(bucket.shape, F32)
    for b in range(NUM_BUCKETS):
        val = jnp.where(bucket == b, rb_ref[b, head], val)
    return val


def _bias_kernel(rb_ref, band_ref, dec_ref, *, past, group_width):
    kk = lax.broadcasted_iota(jnp.int32, (LANES, LANES), 0)
    qq = lax.broadcasted_iota(jnp.int32, (LANES, LANES), 1)
    for d in range(2):
        rel = qq - kk + d * LANES
        bucket = _bucket_of(rel)
        for h in range(N_HEADS):
            val = (_bias_of(bucket, rb_ref, h) - rb_ref[NUM_BUCKETS - 1, h]) * LOG2E
            band_ref[d, h] = jnp.where(rel >= 0, val, NEG)
    rows, width = dec_ref.shape
    row = lax.broadcasted_iota(jnp.int32, (rows, width), 0)
    col = lax.broadcasted_iota(jnp.int32, (rows, width), 1)
    rel = past - ((row >> 3) * group_width + (col >> 2))
    bucket = _bucket_of(rel)
    head = col & (N_HEADS - 1)
    own = head == (row & (N_HEADS - 1))
    val = jnp.zeros((rows, width), F32)
    for h in range(N_HEADS):
        val = jnp.where(head == h, _bias_of(bucket, rb_ref, h), val)
    dec_ref[...] = jnp.where(own, val, NEG)


def _bias_tables(rel_bias, past, group_width):
    n_groups = past // group_width
    return pl.pallas_call(
        functools.partial(_bias_kernel, past=past, group_width=group_width),
        in_specs=[pl.BlockSpec(memory_space=pltpu.SMEM)],
        out_shape=[jax.ShapeDtypeStruct((2, N_HEADS, LANES, LANES), F32),
                   jax.ShapeDtypeStruct((n_groups * 8, group_width * N_HEADS), F32)],
        compiler_params=pltpu.CompilerParams(vmem_limit_bytes=VMEM_LIMIT_BYTES),
        name="bias_tables",
    )(rel_bias)


def _lambda(lq_ref, lam_init):
    lq = lq_ref[...]
    a = jnp.sum(lq[0:1] * lq[1:2], axis=-1, keepdims=True)
    b = jnp.sum(lq[2:3] * lq[3:4], axis=-1, keepdims=True)
    return jnp.exp(a) - jnp.exp(b) + lam_init


def _first_map_lanes(rows):
    lane = lax.broadcasted_iota(jnp.int32, (rows, D_HEAD), 1)
    return lane < DH_A


SOFTMAX_STRIP = 2 * LANES


def _softmax_update(h, s, v_h, m_sc, l_sc, acc_sc):
    m_prev = m_sc[h]
    m_new = jnp.maximum(m_prev, jnp.max(s, axis=0, keepdims=True))
    alpha = jnp.exp2(m_prev - m_new)
    p = jnp.exp2(s - m_new)
    l_sc[h] = alpha * l_sc[h] + jnp.sum(p, axis=0, keepdims=True)
    acc_sc[h] = alpha * acc_sc[h] + _dot_tn(v_h, p.astype(BF16))
    m_sc[h] = m_new


def _attn_kernel(q_ref, k_ref, v_ref, band_ref, lq_ref, na_ref, o_ref,
                 qbd_sc, m_sc, l_sc, acc_sc, s_sc, p_sc, a_sc, *, t, lam_init):
    i = pl.program_id(0)
    nsub = t // LANES
    first_map = _first_map_lanes(t)

    for h in range(N_HEADS):
        q_h = q_ref[:, h * D_HEAD:(h + 1) * D_HEAD]
        zero = jnp.zeros_like(q_h)
        qbd_sc[h, 0:t, :] = jnp.where(first_map, q_h, zero)
        qbd_sc[h, t:2 * t, :] = jnp.where(first_map, zero, q_h)
    m_sc[...] = jnp.full(m_sc.shape, NEG, F32)
    l_sc[...] = jnp.zeros(l_sc.shape, F32)
    acc_sc[...] = jnp.zeros(acc_sc.shape, F32)

    def scores(h, ks):
        k_h = k_ref[pl.ds(ks, t), h * D_HEAD:(h + 1) * D_HEAD]
        v_h = v_ref[pl.ds(ks, t), h * D_HEAD:(h + 1) * D_HEAD]
        return _dot_nt(k_h, qbd_sc[h]), v_h

    def add_band(s, h, lead_of):
        rows = []
        for kc in range(nsub):
            cols = []
            for mp in range(2):
                for qr in range(nsub):
                    sub = s[kc * LANES:(kc + 1) * LANES, mp * t + qr * LANES: mp * t + (qr + 1) * LANES]
                    lead = lead_of(kc, qr)
                    if lead < 0:
                        cols.append(jnp.full_like(sub, NEG))
                    elif lead < 2:
                        cols.append(sub + band_ref[lead, h])
                    else:
                        cols.append(sub)
            rows.append(jnp.concatenate(cols, axis=1))
        return jnp.concatenate(rows, axis=0) if nsub > 1 else rows[0]

    def stage_scores(h, ks, slot):
        s_sc[slot] = _dot_nt(k_ref[pl.ds(ks, t), h * D_HEAD:(h + 1) * D_HEAD], qbd_sc[h])

    def stage_softmax(h, slot):
        m_prev = m_sc[h]
        m_new = jnp.maximum(m_prev, jnp.max(s_sc[slot], axis=0, keepdims=True))
        alpha = jnp.exp2(m_prev - m_new)
        sums = []
        for c0 in range(0, 2 * t, SOFTMAX_STRIP):
            cs = slice(c0, c0 + SOFTMAX_STRIP)
            p = jnp.exp2(s_sc[slot, :, cs] - m_new[:, cs])
            sums.append(jnp.sum(p, axis=0, keepdims=True))
            p_sc[slot, :, cs] = p.astype(BF16)
        l_sc[h] = alpha * l_sc[h] + jnp.concatenate(sums, axis=1)
        a_sc[h] = alpha
        m_sc[h] = m_new

    def stage_values(h, ks, slot):
        v_h = v_ref[pl.ds(ks, t), h * D_HEAD:(h + 1) * D_HEAD]
        acc_sc[h] = a_sc[h] * acc_sc[h] + _dot_tn(v_h, p_sc[slot])

    def far_block(j, carry):
        ks = pl.multiple_of(j * t, t)
        stage_scores(0, ks, 0)
        for h in range(N_HEADS):
            if h + 1 < N_HEADS:
                stage_scores(h + 1, ks, (h + 1) % 2)
            stage_softmax(h, h % 2)
            stage_values(h, ks, h % 2)
        return carry

    lax.fori_loop(0, jnp.maximum(i - 1, 0), far_block, 0)

    @pl.when(i >= 1)
    def _():
        ks = pl.multiple_of((i - 1) * t, t)
        for h in range(N_HEADS):
            s, v_h = scores(h, ks)
            s = add_band(s, h, lambda kc, qr: qr - kc + nsub)
            _softmax_update(h, s, v_h, m_sc, l_sc, acc_sc)

    ks = pl.multiple_of(i * t, t)
    lam = _lambda(lq_ref, lam_init)
    for h in range(N_HEADS):
        s, v_h = scores(h, ks)
        s = add_band(s, h, lambda kc, qr: qr - kc)
        _softmax_update(h, s, v_h, m_sc, l_sc, acc_sc)
        o = acc_sc[h] / l_sc[h]
        attn = o[:, 0:t] - lam * o[:, t:2 * t]
        norm = lax.rsqrt(jnp.mean(attn * attn, axis=0, keepdims=True) + EPS)
        o_ref[h * D_HEAD:(h + 1) * D_HEAD, :] = (
            attn * norm * na_ref[...] * (1.0 - lam_init)).astype(o_ref.dtype)


def _prompt_attention(qkv, band, lambda_qk, norm_a_col, lam_init):
    seq = qkv.shape[0]
    t = min(512, seq)
    assert t % LANES == 0 and seq % t == 0 and t + 1 >= FAR_DISTANCE
    const2 = lambda i: (0, 0)
    return pl.pallas_call(
        functools.partial(_attn_kernel, t=t, lam_init=lam_init),
        grid=(seq // t,),
        in_specs=[pl.BlockSpec((t, W_BRANCH), lambda i: (i, 0)),
                  _resident((seq, W_BRANCH), lambda i: (0, 1)),
                  _resident((seq, W_BRANCH), lambda i: (0, 2)),
                  _resident((2, N_HEADS, LANES, LANES), lambda i: (0, 0, 0, 0)),
                  _resident((4, DH_A), const2),
                  _resident((D_HEAD, 1), const2)],
        out_specs=pl.BlockSpec((W_BRANCH, t), lambda i: (0, i)),
        out_shape=jax.ShapeDtypeStruct((W_BRANCH, seq), BF16),
        scratch_shapes=[pltpu.VMEM((N_HEADS, 2 * t, D_HEAD), BF16),
                        pltpu.VMEM((N_HEADS, 1, 2 * t), F32),
                        pltpu.VMEM((N_HEADS, 1, 2 * t), F32),
                        pltpu.VMEM((N_HEADS, D_HEAD, 2 * t), F32),
                        pltpu.VMEM((2, t, 2 * t), F32),
                        pltpu.VMEM((2, t, 2 * t), BF16),
                        pltpu.VMEM((N_HEADS, 1, 2 * t), F32)],
        compiler_params=_params(),
        name="prompt_attention",
    )(qkv, qkv, qkv, band, lambda_qk, norm_a_col)


DECODE_PAGES_PER_STEP = 16


def _rows_by_head(x_row):
    pieces = [x_row[:, h * D_HEAD:(h + 1) * D_HEAD] for h in range(N_HEADS)]
    return jnp.concatenate(pieces + pieces, axis=0)


def _decode_kernel(pt_ref, ck_hbm, cv_hbm, dec_ref, rb_ref, lq_ref, na_ref, q_ref, ks_ref, vs_ref, o_ref,
                   kbuf, vbuf, sems, qrows_sc, m_sc, l_sc, acc_sc, *, layer, n_pages_step, lam_init):
    b = pl.program_id(0)
    g = pl.program_id(1)
    n_b = pl.num_programs(0)
    n_groups = pl.num_programs(1)
    step = b * n_groups + g
    slot = lax.rem(step, 2)

    def page_copies(bb, gg, sl):
        copies = []
        for r in range(n_pages_step):
            page = pt_ref[bb, gg * n_pages_step + r]
            copies.append(pltpu.make_async_copy(ck_hbm.at[layer, page], kbuf.at[sl, r], sems.at[sl, 0]))
            copies.append(pltpu.make_async_copy(cv_hbm.at[layer, page], vbuf.at[sl, r], sems.at[sl, 1]))
        return copies

    @pl.when(step == 0)
    def _():
        for cp in page_copies(b, g, slot):
            cp.start()

    wraps = g == n_groups - 1
    next_b = jnp.where(wraps, b + 1, b)
    next_g = jnp.where(wraps, 0, g + 1)

    @pl.when(step + 1 < n_b * n_groups)
    def _():
        for cp in page_copies(next_b, next_g, 1 - slot):
            cp.start()

    n_rows = 2 * N_HEADS
    row = lax.broadcasted_iota(jnp.int32, (n_rows, D_HEAD), 0)
    lane = lax.broadcasted_iota(jnp.int32, (n_rows, D_HEAD), 1)
    own_map = (lane >> 6) == (row >> 2)

    @pl.when(g == 0)
    def _():
        q8 = _rows_by_head(q_ref[...] * (DH_A ** -0.5))
        qrows_sc[...] = jnp.where(own_map, q8, 0.0)
        m_sc[...] = jnp.full(m_sc.shape, NEG, F32)
        l_sc[...] = jnp.zeros(l_sc.shape, F32)
        acc_sc[...] = jnp.zeros(acc_sc.shape, F32)

    qrows = qrows_sc[...]
    qb = qrows.astype(BF16)

    for cp in page_copies(b, g, slot):
        cp.wait()
    k_pages = [kbuf[slot, r] for r in range(n_pages_step)]
    v_pages = [vbuf[slot, r] for r in range(n_pages_step)]

    s = jnp.concatenate([_dot_nt(qb, k_pages[r].astype(BF16)) for r in range(n_pages_step)], axis=1)
    s = s + dec_ref[...]
    m_prev = m_sc[...]
    m_new = jnp.maximum(m_prev, jnp.max(s, axis=-1, keepdims=True))
    alpha = jnp.exp(m_prev - m_new)
    p = jnp.exp(s - m_new)
    l_sc[...] = alpha * l_sc[...] + jnp.sum(p, axis=-1, keepdims=True)
    pb = p.astype(BF16)
    pv = None
    for r in range(n_pages_step):
        part = _dot(pb[:, r * PAGE_ROWS:(r + 1) * PAGE_ROWS], v_pages[r].astype(BF16))
        pv = part if pv is None else pv + part
    acc_sc[...] = alpha * acc_sc[...] + pv
    m_sc[...] = m_new

    @pl.when(g == pl.num_programs(1) - 1)
    def _():
        head = lax.broadcasted_iota(jnp.int32, (n_rows, 1), 0) & (N_HEADS - 1)
        b0 = jnp.zeros((n_rows, 1), F32)
        for h in range(N_HEADS):
            b0 = jnp.where(head == h, rb_ref[0, h], b0)
        s_own = jnp.sum(qrows * _rows_by_head(ks_ref[...]), axis=-1, keepdims=True) + b0
        m_prev = m_sc[...]
        m_fin = jnp.maximum(m_prev, s_own)
        alpha = jnp.exp(m_prev - m_fin)
        p_own = jnp.exp(s_own - m_fin)
        l_fin = alpha * l_sc[...] + p_own
        acc = alpha * acc_sc[...] + p_own * _rows_by_head(vs_ref[...])
        o = acc / l_fin
        lam = _lambda(lq_ref, lam_init)
        attn = o[0:N_HEADS] - lam * o[N_HEADS:n_rows]
        out = _rms(attn, na_ref[...]) * (1.0 - lam_init)
        for h in range(N_HEADS):
            o_ref[:, h * D_HEAD:(h + 1) * D_HEAD] = out[h:h + 1].astype(o_ref.dtype)


def _sample_attention(p_s, cache_k, cache_v, layer, page_table, dec_bias, rel_bias, lambda_qk, norm_a,
                      lam_init, n_pages_step):
    db = p_s.shape[0]
    n_pages = page_table.shape[1]
    n_groups = n_pages // n_pages_step
    gw = n_pages_step * PAGE_ROWS
    p3 = p_s.reshape(db, 1, N_MAIN)
    dec3 = dec_bias.reshape(n_groups, 8, gw)

    def row_spec(col):
        return pl.BlockSpec((None, 1, W_BRANCH), lambda b, g, pt: (b, 0, col // W_BRANCH))

    const2 = lambda b, g, pt: (0, 0)
    page_buffers = pltpu.VMEM((2, n_pages_step, PAGE_ROWS, D_HEAD), F32)
    grid_spec = pltpu.PrefetchScalarGridSpec(
        num_scalar_prefetch=1,
        grid=(db, n_groups),
        in_specs=[pl.BlockSpec(memory_space=pl.ANY),
                  pl.BlockSpec(memory_space=pl.ANY),
                  pl.BlockSpec((None, 8, gw), lambda b, g, pt: (g, 0, 0)),
                  pl.BlockSpec(memory_space=pltpu.SMEM),
                  pl.BlockSpec((4, DH_A), const2),
                  pl.BlockSpec((1, D_HEAD), const2),
                  row_spec(C_AQ), row_spec(C_AK), row_spec(C_AV)],
        out_specs=pl.BlockSpec((None, 1, W_BRANCH), lambda b, g, pt: (b, 0, 0)),
        scratch_shapes=[page_buffers, page_buffers,
                        pltpu.SemaphoreType.DMA((2, 2)),
                        pltpu.VMEM((8, D_HEAD), F32),
                        pltpu.VMEM((8, 1), F32),
                        pltpu.VMEM((8, 1), F32),
                        pltpu.VMEM((8, D_HEAD), F32)])
    out = pl.pallas_call(
        functools.partial(_decode_kernel, layer=layer, n_pages_step=n_pages_step, lam_init=lam_init),
        grid_spec=grid_spec,
        out_shape=jax.ShapeDtypeStruct((db, 1, W_BRANCH), BF16),
        compiler_params=_params(2),
        name="sample_attention",
    )(page_table, cache_k, cache_v, dec3, rel_bias, lambda_qk, norm_a, p3, p3, p3)
    return out.reshape(db, W_BRANCH)


def _log_sigmoid(x):
    return jnp.minimum(x, 0.0) - jnp.log1p(jnp.exp(-jnp.abs(x)))


def _split_bf16(x):
    hi = x.astype(BF16)
    lo = (x - hi.astype(F32)).astype(BF16)
    return hi, lo


def _dot_3pass(a, b):
    a_hi, a_lo = _split_bf16(a)
    b_hi, b_lo = _split_bf16(b)
    return _dot(a_hi, b_hi) + (_dot(a_hi, b_lo) + _dot(a_lo, b_hi))


def _gla_log_decay(bg, wup_ref, bgla_ref):
    x = _dot_3pass(bg, wup_ref[...])
    return _log_sigmoid(x + bgla_ref[...]) / GLA_TAU


def _rotate_half(x):
    width = x.shape[-1]
    lane = lax.broadcasted_iota(jnp.int32, x.shape, x.ndim - 1)
    lo = (lane & (DK - 1)) < DK // 2
    return jnp.where(lo, pltpu.roll(x, width - DK // 2, x.ndim - 1), pltpu.roll(x, DK // 2, x.ndim - 1))


def _rotary(x, cos, sin_signed):
    return x * cos + _rotate_half(x) * sin_signed


def _log_gamma(h):
    return math.log(1.0 - 2.0 ** (-5.0 - h))


def _head_of_keys(shape, dim):
    return lax.broadcasted_iota(jnp.int32, shape, dim) >> 6


def _layernorm(x, g):
    xc = x - jnp.mean(x, axis=-1, keepdims=True)
    return xc * lax.rsqrt(jnp.mean(xc * xc, axis=-1, keepdims=True) + EPS) * g


def _block_diag_mask():
    r = lax.broadcasted_iota(jnp.int32, (W_KEYS, W_BRANCH), 0) >> 6
    c = lax.broadcasted_iota(jnp.int32, (W_KEYS, W_BRANCH), 1) >> 7
    return r == c


def _head_blocks(full):
    return jnp.concatenate([full[h * DK:(h + 1) * DK, h * D_HEAD:(h + 1) * D_HEAD]
                            for h in range(N_HEADS)], axis=0)


GROUP = 256


def _linattn_kernel(bq_ref, bk_ref, bv_ref, bg_ref, cq_ref, ck_ref, cv_ref, cos_ref, sin_ref,
                    wup_ref, bgla_ref, nb_ref, nc_ref,
                    bo_ref, co_ref, sg_ref, sr_ref,
                    stg_sc, str_sc, *, rows):
    step = pl.program_id(0)

    @pl.when(step == 0)
    def _():
        stg_sc[...] = jnp.zeros(stg_sc.shape, F32)
        str_sc[...] = jnp.zeros(str_sc.shape, F32)

    n_chunks = rows // CHUNK
    n_groups = rows // GROUP
    cpg = GROUP // CHUNK
    ti = lax.broadcasted_iota(jnp.int32, (GROUP, GROUP), 0)
    si = lax.broadcasted_iota(jnp.int32, (GROUP, GROUP), 1)
    causal = ((ti >> 6) == (si >> 6)) & (ti >= si)
    tril = jnp.where(causal, 1.0, 0.0).astype(BF16)
    dist = (ti - si).astype(F32)
    key_head = _head_of_keys((GROUP, W_KEYS), 1)

    tcol = (lax.broadcasted_iota(jnp.int32, (GROUP, W_KEYS), 0) & (CHUNK - 1)).astype(F32)
    lg_lane = jnp.zeros((GROUP, W_KEYS), F32)
    lg_row = jnp.zeros((W_KEYS, D_HEAD), F32)
    row_head = _head_of_keys((W_KEYS, D_HEAD), 0)
    for h in range(N_HEADS):
        lg_lane = jnp.where(key_head == h, _log_gamma(h), lg_lane)
        lg_row = jnp.where(row_head == h, _log_gamma(h), lg_row)
    g_fwd = jnp.exp((tcol + 1.0) * lg_lane)
    g_rev = jnp.exp((CHUNK - 1.0 - tcol) * lg_lane)
    g_end = jnp.exp(CHUNK * lg_row)

    glog = _gla_log_decay(bg_ref[...], wup_ref, bgla_ref)
    cq_all = _rotary(cq_ref[...], cos_ref[...], sin_ref[...])
    ck_all = _rotary(ck_ref[...], cos_ref[...], sin_ref[...]) * (DK ** -0.5)
    bq_all = bq_ref[...] * (DK ** -0.5)

    g_qd, g_oi, g_upd, g_end_rows = [], [], [], []
    r_qf, r_oi, r_upd = [], [], []
    for g in range(n_groups):
        rs = slice(g * GROUP, (g + 1) * GROUP)
        g_hi, g_lo = _split_bf16(glog[rs])
        gcum = _dot(tril, g_hi) + _dot(tril, g_lo)
        ends = [gcum[(c + 1) * CHUNK - 1:(c + 1) * CHUNK] for c in range(cpg)]
        g_end_rows += ends
        gend_b = jnp.concatenate([jnp.broadcast_to(e, (CHUNK, W_KEYS)) for e in ends], axis=0)
        qd = bq_all[rs] * jnp.exp(gcum)
        bk = bk_ref[rs, :]
        kinv = (bk * jnp.exp(-gcum)).astype(BF16)
        krem = (bk * jnp.exp(gend_b - gcum)).astype(BF16)
        v = bv_ref[rs, :].astype(BF16)
        parts = []
        for h in range(N_HEADS):
            qm = jnp.where(key_head == h, qd, 0.0).astype(BF16)
            sc = jnp.where(causal, _dot_nt(qm, kinv), 0.0)
            parts.append(_dot(sc.astype(BF16), v[:, h * D_HEAD:(h + 1) * D_HEAD]))
        oi = jnp.concatenate(parts, axis=1)
        qdb = qd.astype(BF16)
        for c in range(cpg):
            cs = slice(c * CHUNK, (c + 1) * CHUNK)
            g_qd.append(qdb[cs])
            g_oi.append(oi[cs])
            g_upd.append(_head_blocks(_dot_tn(krem[cs], v[cs])))
        q = cq_all[rs]
        k = ck_all[rs]
        v = cv_ref[rs, :].astype(BF16)
        kb = k.astype(BF16)
        parts = []
        for h in range(N_HEADS):
            qm = jnp.where(key_head == h, q, 0.0).astype(BF16)
            decay = jnp.where(causal, jnp.exp(dist * _log_gamma(h)), 0.0)
            sc = _dot_nt(qm, kb) * decay
            parts.append(_dot(sc.astype(BF16), v[:, h * D_HEAD:(h + 1) * D_HEAD]))
        oi = jnp.concatenate(parts, axis=1)
        krev = (k * g_rev).astype(BF16)
        qfb = (q * g_fwd).astype(BF16)
        for c in range(cpg):
            cs = slice(c * CHUNK, (c + 1) * CHUNK)
            r_qf.append(qfb[cs])
            r_oi.append(oi[cs])
            r_upd.append(_head_blocks(_dot_tn(krev[cs], v[cs])))

    e_cols = jnp.exp(jnp.transpose(jnp.concatenate(g_end_rows, axis=0)))

    bd = _block_diag_mask()

    def state_products(qs, st):
        stb = st.astype(BF16)
        st_bd = jnp.where(bd, jnp.concatenate([stb] * N_HEADS, axis=1), jnp.zeros((), BF16))
        return _dot(qs, st_bd)

    st = stg_sc[...]
    for c in range(n_chunks):
        rs = slice(c * CHUNK, (c + 1) * CHUNK)
        o = g_oi[c] + state_products(g_qd[c], st)
        st = e_cols[:, c:c + 1] * st + g_upd[c]
        for h in range(N_HEADS):
            hs = slice(h * D_HEAD, (h + 1) * D_HEAD)
            bo_ref[rs, hs] = _rms(o[:, hs], nb_ref[:, hs]).astype(bo_ref.dtype)
    stg_sc[...] = st
    st = str_sc[...]
    for c in range(n_chunks):
        rs = slice(c * CHUNK, (c + 1) * CHUNK)
        o = r_oi[c] + state_products(r_qf[c], st)
        st = g_end * st + r_upd[c]
        for h in range(N_HEADS):
            hs = slice(h * D_HEAD, (h + 1) * D_HEAD)
            co_ref[rs, hs] = _layernorm(o[:, hs], nc_ref[:, hs]).astype(co_ref.dtype)
    str_sc[...] = st

    @pl.when(step == pl.num_programs(0) - 1)
    def _():
        for h in range(N_HEADS):
            sg_ref[h] = stg_sc[h * DK:(h + 1) * DK, :]
            sr_ref[h] = str_sc[h * DK:(h + 1) * DK, :]


def _prompt_linear_attention(p, cos, sin, wup, bgla, norm_b, norm_c):
    seq = p.shape[0]
    rows = min(512, seq)

    def col(c, w):
        return pl.BlockSpec((rows, w), lambda i: (i, c // w))

    const = lambda i: (0, 0)
    state = jax.ShapeDtypeStruct((N_HEADS, DK, D_HEAD), F32)
    state_spec = pl.BlockSpec((N_HEADS, DK, D_HEAD), lambda i: (0, 0, 0))
    return pl.pallas_call(
        functools.partial(_linattn_kernel, rows=rows),
        grid=(seq // rows,),
        in_specs=[col(C_BQ, W_KEYS), col(C_BK, W_KEYS), col(C_BV, W_BRANCH), col(C_BG, LANES),
                  col(C_CQ, W_KEYS), col(C_CK, W_KEYS), col(C_CV, W_BRANCH),
                  pl.BlockSpec((rows, W_KEYS), lambda i: (i, 0)),
                  pl.BlockSpec((rows, W_KEYS), lambda i: (i, 0)),
                  pl.BlockSpec((LANES, W_KEYS), const),
                  pl.BlockSpec((1, W_KEYS), const),
                  pl.BlockSpec((1, W_BRANCH), const),
                  pl.BlockSpec((1, W_BRANCH), const)],
        out_specs=[pl.BlockSpec((rows, W_BRANCH), lambda i: (i, 0)),
                   pl.BlockSpec((rows, W_BRANCH), lambda i: (i, 0)),
                   state_spec, state_spec],
        out_shape=[jax.ShapeDtypeStruct((seq, W_BRANCH), BF16),
                   jax.ShapeDtypeStruct((seq, W_BRANCH), BF16),
                   state, state],
        scratch_shapes=[pltpu.VMEM((W_KEYS, D_HEAD), F32),
                        pltpu.VMEM((W_KEYS, D_HEAD), F32)],
        compiler_params=_params(),
        name="prompt_linear_attention",
    )(p, p, p, p, p, p, p, cos, sin, wup, bgla, norm_b, norm_c)


def _to_column(x_row):
    n = x_row.shape[-1]
    r = lax.broadcasted_iota(jnp.int32, (n, n), 0)
    c = lax.broadcasted_iota(jnp.int32, (n, n), 1)
    return jnp.sum(jnp.where(r == c, jnp.broadcast_to(x_row, (n, n)), 0.0), axis=-1, keepdims=True)


def _recurrent_kernel(bq_ref, bk_ref, bv_ref, bg_ref, cq_ref, ck_ref, cv_ref, cos_ref, sin_ref,
                      wup_ref, bgla_ref, nb_ref, nc_ref, sg0_ref, sr0_ref,
                      bo_ref, co_ref, sg_ref, sr_ref):
    decay = _to_column(jnp.exp(_gla_log_decay(bg_ref[...], wup_ref, bgla_ref)))
    bq = _to_column(bq_ref[...] * (DK ** -0.5))
    bk = _to_column(bk_ref[...])
    cq = _to_column(_rotary(cq_ref[...], cos_ref[...], sin_ref[...]))
    ck = _to_column(_rotary(ck_ref[...], cos_ref[...], sin_ref[...]) * (DK ** -0.5))
    for h in range(N_HEADS):
        ks = slice(h * DK, (h + 1) * DK)
        hs = slice(h * D_HEAD, (h + 1) * D_HEAD)
        s_new = decay[ks] * sg0_ref[h] + bk[ks] * bv_ref[:, hs]
        sg_ref[h] = s_new
        o = jnp.sum(bq[ks] * s_new, axis=0, keepdims=True)
        bo_ref[:, hs] = _rms(o, nb_ref[:, hs]).astype(bo_ref.dtype)
        s_new = math.exp(_log_gamma(h)) * sr0_ref[h] + ck[ks] * cv_ref[:, hs]
        sr_ref[h] = s_new
        o = jnp.sum(cq[ks] * s_new, axis=0, keepdims=True)
        co_ref[:, hs] = _layernorm(o, nc_ref[:, hs]).astype(co_ref.dtype)


def _sample_linear_attention(p_s, cos, sin, wup, bgla, norm_b, norm_c, state_gla, state_ret, layer):
    db = p_s.shape[0]
    p3 = p_s.reshape(db, 1, N_MAIN)

    def col(c, w):
        return pl.BlockSpec((None, 1, w), lambda b: (b, 0, c // w))

    const = lambda b: (0, 0)
    state_spec = pl.BlockSpec((None, N_HEADS, DK, D_HEAD), lambda b: (b, 0, 0, 0))
    state_in_spec = pl.BlockSpec((None, None, N_HEADS, DK, D_HEAD), lambda b: (layer, b, 0, 0, 0))
    out_row = pl.BlockSpec((None, 1, W_BRANCH), lambda b: (b, 0, 0))
    state = jax.ShapeDtypeStruct((db, N_HEADS, DK, D_HEAD), F32)
    bo, co, sg, sr = pl.pallas_call(
        _recurrent_kernel,
        grid=(db,),
        in_specs=[col(C_BQ, W_KEYS), col(C_BK, W_KEYS), col(C_BV, W_BRANCH), col(C_BG, LANES),
                  col(C_CQ, W_KEYS), col(C_CK, W_KEYS), col(C_CV, W_BRANCH),
                  pl.BlockSpec((1, W_KEYS), const), pl.BlockSpec((1, W_KEYS), const),
                  pl.BlockSpec((LANES, W_KEYS), const),
                  pl.BlockSpec((1, W_KEYS), const),
                  pl.BlockSpec((1, W_BRANCH), const),
                  pl.BlockSpec((1, W_BRANCH), const),
                  state_in_spec, state_in_spec],
        out_specs=[out_row, out_row, state_spec, state_spec],
        out_shape=[jax.ShapeDtypeStruct((db, 1, W_BRANCH), BF16),
                   jax.ShapeDtypeStruct((db, 1, W_BRANCH), BF16),
                   state, state],
        compiler_params=_params(),
        name="sample_linear_attention",
    )(p3, p3, p3, p3, p3, p3, p3, cos, sin, wup, bgla, norm_b, norm_c, state_gla, state_ret)
    return bo.reshape(db, W_BRANCH), co.reshape(db, W_BRANCH), sg, sr


def _merge_kernel(x_ref, aot_ref, bo_ref, co_ref, br_ref, cg_ref, gates_ref,
                  wa_ref, wb_ref, wc_ref, wo_ref, o_ref):
    ya = _dot_tn(aot_ref[...], wa_ref[...])
    br = br_ref[...].astype(F32)
    yb = _dot((bo_ref[...].astype(F32) * (br * _sigmoid(br))).astype(BF16), wb_ref[...])
    cg = cg_ref[...].astype(F32)
    yc = _dot((co_ref[...].astype(F32) * (cg * _sigmoid(cg))).astype(BF16), wc_ref[...])
    mix = (_sigmoid(gates_ref[:, 0:D_MODEL].astype(F32)) * ya
           + _sigmoid(gates_ref[:, D_MODEL:2 * D_MODEL].astype(F32)) * yb
           + _sigmoid(gates_ref[:, 2 * D_MODEL:3 * D_MODEL].astype(F32)) * yc)
    o_ref[...] = x_ref[...] + _dot(mix.astype(BF16), wo_ref[...])


def _merge(x, aot, bo, co, p, wa, wb, wc, wo, layer):
    m = x.shape[0]
    tm = min(512, m)
    row = lambda i: (i, 0)
    of_layer = lambda i: (layer, 0, 0)
    return pl.pallas_call(
        _merge_kernel,
        grid=(m // tm,),
        in_specs=[pl.BlockSpec((tm, D_MODEL), row),
                  pl.BlockSpec((W_BRANCH, tm), lambda i: (0, i)),
                  pl.BlockSpec((tm, W_BRANCH), row),
                  pl.BlockSpec((tm, W_BRANCH), row),
                  pl.BlockSpec((tm, W_BRANCH), lambda i: (i, C_BR // W_BRANCH)),
                  pl.BlockSpec((tm, W_BRANCH), lambda i: (i, C_CG // W_BRANCH)),
                  pl.BlockSpec((tm, 3 * D_MODEL), lambda i: (i, C_GATES // (3 * D_MODEL))),
                  _resident((None, W_BRANCH, D_MODEL), of_layer),
                  _resident((None, W_BRANCH, D_MODEL), of_layer),
                  _resident((None, W_BRANCH, D_MODEL), of_layer),
                  _resident((None, D_MODEL, D_MODEL), of_layer)],
        out_specs=pl.BlockSpec((tm, D_MODEL), row),
        out_shape=jax.ShapeDtypeStruct((m, D_MODEL), F32),
        compiler_params=_params(),
        name="merge",
    )(x, aot, bo, co, p, p, p, wa, wb, wc, wo)


def _rope_tables(pos):
    half = DK // 2
    inv = ROPE_BASE ** (-jnp.arange(half, dtype=F32) / half)
    ang = pos.astype(F32)[:, None] * inv[None, :]
    cos = jnp.cos(ang)
    sin = jnp.sin(ang)
    cos_t = jnp.tile(jnp.concatenate([cos, cos], axis=-1), (1, N_HEADS))
    sin_t = jnp.tile(jnp.concatenate([-sin, sin], axis=-1), (1, N_HEADS))
    return cos_t, sin_t


def _reorder_w_in(w):
    s_a = 3 * W_BRANCH
    s_b = s_a + 2 * W_KEYS + 2 * W_BRANCH
    s_g = s_b + GLA_RANK
    s_c = s_g + 2 * W_KEYS + 2 * W_BRANCH
    s_br = s_b - W_BRANCH
    s_cg = s_c - W_BRANCH
    pad = jnp.zeros(w.shape[:-1] + (N_MAIN - C_BG - GLA_RANK,), w.dtype)
    return jnp.concatenate([w[..., s_c:], w[..., s_br:s_b], w[..., s_cg:s_c],
                            w[..., :s_br], w[..., s_g:s_cg], w[..., s_b:s_g], pad], axis=-1)


def kernel(x_prompt, x_sample, cache_k, cache_v, state_gla, state_ret, page_table, w_in, w_gla_up, b_gla, lambda_qk, rel_bias, norm_a, norm_b, norm_c, w_a, w_b, w_c, w_out, norm_ffn1, ffn1_wg, ffn1_wu, ffn1_wd, norm_mix, norm_ffn2, ffn2_wg, ffn2_wu, ffn2_wd, norm_final):
    batch, seq, _ = x_prompt.shape
    db, dec_seq, _ = x_sample.shape
    depth = w_in.shape[0]
    assert batch == 1 and dec_seq == 1
    n_pages = page_table.shape[1]
    past = n_pages * PAGE_SIZE
    n_pool = cache_k.shape[1]
    n_pages_step = math.gcd(DECODE_PAGES_PER_STEP, n_pages)

    band, dec_bias = _bias_tables(rel_bias, past, n_pages_step * PAGE_SIZE)
    cos_p, sin_p = _rope_tables(jnp.arange(seq))
    cos_s, sin_s = _rope_tables(past + jnp.arange(dec_seq))
    ck4 = cache_k.reshape(depth, n_pool, PAGE_ROWS, D_HEAD)
    cv4 = cache_v.reshape(depth, n_pool, PAGE_ROWS, D_HEAD)
    g_final = norm_final.reshape(1, D_MODEL)

    xp = x_prompt.reshape(seq, D_MODEL)
    xs = x_sample.reshape(db, D_MODEL)
    w_in_b = _reorder_w_in(w_in).astype(BF16)
    ffn1_w = tuple(w.astype(BF16) for w in (ffn1_wg, ffn1_wu, ffn1_wd))
    ffn2_w = tuple(w.astype(BF16) for w in (ffn2_wg, ffn2_wu, ffn2_wd))
    proj_w = tuple(w.astype(BF16) for w in (w_a, w_b, w_c, w_out))
    outs = {k: [] for k in ("kp", "vp", "gp", "rp", "ks", "vs", "gs", "rs")}
    for l in range(depth):
        lam_init = 0.8 - 0.6 * math.exp(-0.3 * l)
        last = l == depth - 1
        wup = jnp.concatenate([w_gla_up[l], jnp.zeros((LANES - GLA_RANK, W_KEYS), F32)], axis=0)
        bgla = b_gla[l].reshape(1, W_KEYS)
        nb = norm_b[l].reshape(1, W_BRANCH)
        nc = norm_c[l].reshape(1, W_BRANCH)
        na = norm_a[l].reshape(1, D_HEAD)
        na_col = norm_a[l].reshape(D_HEAD, 1)
        ffn1 = (norm_ffn1[l].reshape(1, D_MODEL), *ffn1_w, l)
        ffn2 = (norm_ffn2[l].reshape(1, D_MODEL), *ffn2_w, l)
        g_mix = norm_mix[l].reshape(1, D_MODEL)

        xp = _ffn_half(xp, *ffn1, g_final, False)
        p, gate_cols, qkv, k_new, v_new = _inproj(xp, g_mix, w_in_b, l)
        aot = _prompt_attention(qkv, band, lambda_qk[l], na_col, lam_init)
        bo, co, sg, sr = _prompt_linear_attention(p, cos_p, sin_p, wup, bgla, nb, nc)
        xp = _merge(xp, aot, bo, co, gate_cols, *proj_w, l)
        xp = _ffn_half(xp, *ffn2, g_final, last)
        outs["kp"].append(k_new[None])
        outs["vp"].append(v_new[None])
        outs["gp"].append(sg[None])
        outs["rp"].append(sr[None])

        xs = _ffn_half(xs, *ffn1, g_final, False)
        p_s, gate_cols, _, k_new, v_new = _inproj(xs, g_mix, w_in_b, l)
        ao = _sample_attention(p_s, ck4, cv4, l, page_table, dec_bias, rel_bias, lambda_qk[l], na,
                               lam_init, n_pages_step)
        bo, co, sg, sr = _sample_linear_attention(p_s, cos_s, sin_s, wup, bgla, nb, nc,
                                                  state_gla, state_ret, l)
        xs = _merge(xs, ao.T, bo, co, gate_cols, *proj_w, l)
        xs = _ffn_half(xs, *ffn2, g_final, last)
        outs["ks"].append(k_new[:, None])
        outs["vs"].append(v_new[:, None])
        outs["gs"].append(sg)
        outs["rs"].append(sr)

    st = lambda key: jnp.stack(outs[key])
    return (xp.reshape(batch, seq, D_MODEL), xs.reshape(db, dec_seq, D_MODEL),
            st("kp"), st("vp"), st("gp"), st("rp"), st("ks"), st("vs"), st("gs"), st("rs"))
```

```python
import functools
import math

import jax
import jax.numpy as jnp
from jax import lax
from jax.experimental import pallas as pl
from jax.experimental.pallas import tpu as pltpu

F32 = jnp.float32
BF16 = jnp.bfloat16

D_MODEL = 1024
N_HEADS = 4
DH_A = 64
D_HEAD = 128
DK = 64
GLA_RANK = 16
GLA_TAU = 16.0
ROPE_BASE = 10000.0
NUM_BUCKETS = 32
MAX_DISTANCE = 128
D_FF = 2816
PAGE_SIZE = 128
CHUNK = 64
EPS = 1e-6
NEG = -1e30
LOG2E = math.log2(math.e)
W_BRANCH = N_HEADS * D_HEAD
W_KEYS = N_HEADS * DK
PAGE_ROWS = PAGE_SIZE * N_HEADS

LANES = 128
VMEM_LIMIT_BYTES = 56 * 1024 * 1024

C_GATES = 0
C_BR = 3072
C_CG = 3584
N_GATE = 4096
C_AQ = 0
C_AK = 512
C_AV = 1024
C_BQ = 1536
C_BK = 1792
C_BV = 2048
C_CQ = 2560
C_CK = 2816
C_CV = 3072
C_BG = 3584
N_MAIN = 3712
N_PROJ = N_GATE + N_MAIN

FAR_DISTANCE = 113


def _params(n_axes=1, vmem=VMEM_LIMIT_BYTES):
    return pltpu.CompilerParams(dimension_semantics=("arbitrary",) * n_axes,
                                vmem_limit_bytes=vmem)


def _resident(shape, index_map):
    return pl.BlockSpec(shape, index_map, pipeline_mode=pl.Buffered(1))


def _rms(x, g):
    return x * lax.rsqrt(jnp.mean(x * x, axis=-1, keepdims=True) + EPS) * g


def _sigmoid(x):
    return 1.0 / (1.0 + jnp.exp(-x))


def _dot(a, b):
    return jnp.dot(a, b, preferred_element_type=F32)


def _dot_nt(a, b):
    return lax.dot_general(a, b, (((1,), (1,)), ((), ())), preferred_element_type=F32)


def _dot_tn(a, b):
    return lax.dot_general(a, b, (((0,), (0,)), ((), ())), preferred_element_type=F32)


FF_CHUNK = 1408


def _ffn_kernel(x_ref, g_ref, wg_ref, wu_ref, wd_ref, gf_ref, o_ref, *, final_norm):
    x = x_ref[...]
    n = _rms(x, g_ref[...]).astype(BF16)
    acc = None
    for c in range(0, D_FF, FF_CHUNK):
        a = _dot(n, wg_ref[:, c:c + FF_CHUNK])
        u = _dot(n, wu_ref[:, c:c + FF_CHUNK])
        h = (a * _sigmoid(a) * u).astype(BF16)
        part = _dot(h, wd_ref[c:c + FF_CHUNK, :])
        acc = part if acc is None else acc + part
    y = x + 0.5 * acc
    if final_norm:
        y = _rms(y, gf_ref[...])
    o_ref[...] = y


def _ffn_half(x, g, wg, wu, wd, layer, g_final, final_norm):
    m = x.shape[0]
    tm = min(512, m)
    row = lambda i: (i, 0)
    const = lambda i: (0, 0)
    of_layer = lambda i: (layer, 0, 0)
    return pl.pallas_call(
        functools.partial(_ffn_kernel, final_norm=final_norm),
        grid=(m // tm,),
        in_specs=[pl.BlockSpec((tm, D_MODEL), row),
                  _resident((1, D_MODEL), const),
                  _resident((None, D_MODEL, D_FF), of_layer),
                  _resident((None, D_MODEL, D_FF), of_layer),
                  _resident((None, D_FF, D_MODEL), of_layer),
                  _resident((1, D_MODEL), const)],
        out_specs=pl.BlockSpec((tm, D_MODEL), row),
        out_shape=jax.ShapeDtypeStruct((m, D_MODEL), F32),
        compiler_params=_params(),
        name="ffn_half",
    )(x, g, wg, wu, wd, g_final)


PROJ_CHUNK = 1024


def _inproj_kernel(x_ref, g_ref, w_ref, p_ref, gate_ref, qkv_ref, k_ref, v_ref):
    n = _rms(x_ref[...], g_ref[...]).astype(BF16)
    for c in range(0, N_GATE, PROJ_CHUNK):
        gate_ref[:, c:c + PROJ_CHUNK] = _dot(n, w_ref[:, c:c + PROJ_CHUNK]).astype(BF16)
    for c in range(0, N_MAIN, PROJ_CHUNK):
        w = min(PROJ_CHUNK, N_MAIN - c)
        p_ref[:, c:c + w] = _dot(n, w_ref[:, N_GATE + c:N_GATE + c + w])
    qkv_ref[:, 0:W_BRANCH] = (p_ref[:, C_AQ:C_AQ + W_BRANCH] * (DH_A ** -0.5 * LOG2E)).astype(BF16)
    qkv_ref[:, W_BRANCH:3 * W_BRANCH] = p_ref[:, C_AK:C_AK + 2 * W_BRANCH].astype(BF16)
    for h in range(N_HEADS):
        k_ref[:, h, :] = p_ref[:, C_AK + h * D_HEAD:C_AK + (h + 1) * D_HEAD]
        v_ref[:, h, :] = p_ref[:, C_AV + h * D_HEAD:C_AV + (h + 1) * D_HEAD]


def _inproj(x, g, w, layer):
    m = x.shape[0]
    tm = min(256, m)
    row = lambda i: (i, 0)
    const = lambda i: (0, 0)
    rows3 = pl.BlockSpec((tm, N_HEADS, D_HEAD), lambda i: (i, 0, 0))
    return pl.pallas_call(
        _inproj_kernel,
        grid=(m // tm,),
        in_specs=[pl.BlockSpec((tm, D_MODEL), row),
                  _resident((1, D_MODEL), const),
                  _resident((None, D_MODEL, N_PROJ), lambda i: (layer, 0, 0))],
        out_specs=[pl.BlockSpec((tm, N_MAIN), row),
                   pl.BlockSpec((tm, N_GATE), row),
                   pl.BlockSpec((tm, 3 * W_BRANCH), row),
                   rows3, rows3],
        out_shape=[jax.ShapeDtypeStruct((m, N_MAIN), F32),
                   jax.ShapeDtypeStruct((m, N_GATE), BF16),
                   jax.ShapeDtypeStruct((m, 3 * W_BRANCH), BF16),
                   jax.ShapeDtypeStruct((m, N_HEADS, D_HEAD), F32),
                   jax.ShapeDtypeStruct((m, N_HEADS, D_HEAD), F32)],
        compiler_params=_params(),
        name="inproj",
    )(x, g, w)


def _bucket_of(rel):
    n = jnp.maximum(rel, 0)
    max_exact = NUM_BUCKETS // 2
    nf = jnp.maximum(n, max_exact).astype(F32)
    large = max_exact + (jnp.log(nf / max_exact) / math.log(MAX_DISTANCE / max_exact)
                         * (NUM_BUCKETS - max_exact)).astype(jnp.int32)
    large = jnp.minimum(large, NUM_BUCKETS - 1)
    return jnp.where(n < max_exact, n, large)


def _bias_of(bucket, rb_ref, head):
    val = jnp.zeros(bucket.shape, F32)
    for b in range(NUM_BUCKETS):
        val = jnp.where(bucket == b, rb_ref[b, head], val)
    return val


def _bias_kernel(rb_ref, band_ref, dec_ref, *, past, group_width):
    kk = lax.broadcasted_iota(jnp.int32, (LANES, LANES), 0)
    qq = lax.broadcasted_iota(jnp.int32, (LANES, LANES), 1)
    for d in range(2):
        rel = qq - kk + d * LANES
        bucket = _bucket_of(rel)
        for h in range(N_HEADS):
            val = (_bias_of(bucket, rb_ref, h) - rb_ref[NUM_BUCKETS - 1, h]) * LOG2E
            band_ref[d, h] = jnp.where(rel >= 0, val, NEG)
    rows, width = dec_ref.shape
    row = lax.broadcasted_iota(jnp.int32, (rows, width), 0)
    col = lax.broadcasted_iota(jnp.int32, (rows, width), 1)
    rel = past - ((row >> 3) * group_width + (col >> 2))
    bucket = _bucket_of(rel)
    head = col & (N_HEADS - 1)
    own = head == (row & (N_HEADS - 1))
    val = jnp.zeros((rows, width), F32)
    for h in range(N_HEADS):
        val = jnp.where(head == h, _bias_of(bucket, rb_ref, h), val)
    dec_ref[...] = jnp.where(own, val, NEG)


def _bias_tables(rel_bias, past, group_width):
    n_groups = past // group_width
    return pl.pallas_call(
        functools.partial(_bias_kernel, past=past, group_width=group_width),
        in_specs=[pl.BlockSpec(memory_space=pltpu.SMEM)],
        out_shape=[jax.ShapeDtypeStruct((2, N_HEADS, LANES, LANES), F32),
                   jax.ShapeDtypeStruct((n_groups * 8, group_width * N_HEADS), F32)],
        compiler_params=pltpu.CompilerParams(vmem_limit_bytes=VMEM_LIMIT_BYTES),
        name="bias_tables",
    )(rel_bias)


def _lambda(lq_ref, lam_init):
    lq = lq_ref[...]
    a = jnp.sum(lq[0:1] * lq[1:2], axis=-1, keepdims=True)
    b = jnp.sum(lq[2:3] * lq[3:4], axis=-1, keepdims=True)
    return jnp.exp(a) - jnp.exp(b) + lam_init


def _first_map_lanes(rows):
    lane = lax.broadcasted_iota(jnp.int32, (rows, D_HEAD), 1)
    return lane < DH_A


SOFTMAX_STRIP = 2 * LANES


def _softmax_update(h, s, v_h, m_sc, l_sc, acc_sc):
    m_prev = m_sc[h]
    m_new = jnp.maximum(m_prev, jnp.max(s, axis=0, keepdims=True))
    alpha = jnp.exp2(m_prev - m_new)
    p = jnp.exp2(s - m_new)
    l_sc[h] = alpha * l_sc[h] + jnp.sum(p, axis=0, keepdims=True)
    acc_sc[h] = alpha * acc_sc[h] + _dot_tn(v_h, p.astype(BF16))
    m_sc[h] = m_new


def _attn_kernel(q_ref, k_ref, v_ref, band_ref, lq_ref, na_ref, o_ref,
                 qbd_sc, m_sc, l_sc, acc_sc, s_sc, p_sc, a_sc, *, t, lam_init):
    i = pl.program_id(0)
    nsub = t // LANES
    first_map = _first_map_lanes(t)

    for h in range(N_HEADS):
        q_h = q_ref[:, h * D_HEAD:(h + 1) * D_HEAD]
        zero = jnp.zeros_like(q_h)
        qbd_sc[h, 0:t, :] = jnp.where(first_map, q_h, zero)
        qbd_sc[h, t:2 * t, :] = jnp.where(first_map, zero, q_h)
    m_sc[...] = jnp.full(m_sc.shape, NEG, F32)
    l_sc[...] = jnp.zeros(l_sc.shape, F32)
    acc_sc[...] = jnp.zeros(acc_sc.shape, F32)

    def scores(h, ks):
        k_h = k_ref[pl.ds(ks, t), h * D_HEAD:(h + 1) * D_HEAD]
        v_h = v_ref[pl.ds(ks, t), h * D_HEAD:(h + 1) * D_HEAD]
        return _dot_nt(k_h, qbd_sc[h]), v_h

    def add_band(s, h, lead_of):
        rows = []
        for kc in range(nsub):
            cols = []
            for mp in range(2):
                for qr in range(nsub):
                    sub = s[kc * LANES:(kc + 1) * LANES, mp * t + qr * LANES: mp * t + (qr + 1) * LANES]
                    lead = lead_of(kc, qr)
                    if lead < 0:
                        cols.append(jnp.full_like(sub, NEG))
                    elif lead < 2:
                        cols.append(sub + band_ref[lead, h])
                    else:
                        cols.append(sub)
            rows.append(jnp.concatenate(cols, axis=1))
        return jnp.concatenate(rows, axis=0) if nsub > 1 else rows[0]

    def stage_scores(h, ks, slot):
        s_sc[slot] = _dot_nt(k_ref[pl.ds(ks, t), h * D_HEAD:(h + 1) * D_HEAD], qbd_sc[h])

    def stage_softmax(h, slot):
        m_prev = m_sc[h]
        m_new = jnp.maximum(m_prev, jnp.max(s_sc[slot], axis=0, keepdims=True))
        alpha = jnp.exp2(m_prev - m_new)
        sums = []
        for c0 in range(0, 2 * t, SOFTMAX_STRIP):
            cs = slice(c0, c0 + SOFTMAX_STRIP)
            p = jnp.exp2(s_sc[slot, :, cs] - m_new[:, cs])
            sums.append(jnp.sum(p, axis=0, keepdims=True))
            p_sc[slot, :, cs] = p.astype(BF16)
        l_sc[h] = alpha * l_sc[h] + jnp.concatenate(sums, axis=1)
        a_sc[h] = alpha
        m_sc[h] = m_new

    def stage_values(h, ks, slot):
        v_h = v_ref[pl.ds(ks, t), h * D_HEAD:(h + 1) * D_HEAD]
        acc_sc[h] = a_sc[h] * acc_sc[h] + _dot_tn(v_h, p_sc[slot])

    def far_block(j, carry):
        ks = pl.multiple_of(j * t, t)
        stage_scores(0, ks, 0)
        for h in range(N_HEADS):
            if h + 1 < N_HEADS:
                stage_scores(h + 1, ks, (h + 1) % 2)
            stage_softmax(h, h % 2)
            stage_values(h, ks, h % 2)
        return carry

    lax.fori_loop(0, jnp.maximum(i - 1, 0), far_block, 0)

    @pl.when(i >= 1)
    def _():
        ks = pl.multiple_of((i - 1) * t, t)
        for h in range(N_HEADS):
            s, v_h = scores(h, ks)
            s = add_band(s, h, lambda kc, qr: qr - kc + nsub)
            _softmax_update(h, s, v_h, m_sc, l_sc, acc_sc)

    ks = pl.multiple_of(i * t, t)
    lam = _lambda(lq_ref, lam_init)
    for h in range(N_HEADS):
        s, v_h = scores(h, ks)
        s = add_band(s, h, lambda kc, qr: qr - kc)
        _softmax_update(h, s, v_h, m_sc, l_sc, acc_sc)
        o = acc_sc[h] / l_sc[h]
        attn = o[:, 0:t] - lam * o[:, t:2 * t]
        norm = lax.rsqrt(jnp.mean(attn * attn, axis=0, keepdims=True) + EPS)
        o_ref[h * D_HEAD:(h + 1) * D_HEAD, :] = (
            attn * norm * na_ref[...] * (1.0 - lam_init)).astype(o_ref.dtype)


def _prompt_attention(qkv, band, lambda_qk, norm_a_col, lam_init):
    seq = qkv.shape[0]
    t = min(512, seq)
    assert t % LANES == 0 and seq % t == 0 and t + 1 >= FAR_DISTANCE
    const2 = lambda i: (0, 0)
    return pl.pallas_call(
        functools.partial(_attn_kernel, t=t, lam_init=lam_init),
        grid=(seq // t,),
        in_specs=[pl.BlockSpec((t, W_BRANCH), lambda i: (i, 0)),
                  _resident((seq, W_BRANCH), lambda i: (0, 1)),
                  _resident((seq, W_BRANCH), lambda i: (0, 2)),
                  _resident((2, N_HEADS, LANES, LANES), lambda i: (0, 0, 0, 0)),
                  _resident((4, DH_A), const2),
                  _resident((D_HEAD, 1), const2)],
        out_specs=pl.BlockSpec((W_BRANCH, t), lambda i: (0, i)),
        out_shape=jax.ShapeDtypeStruct((W_BRANCH, seq), BF16),
        scratch_shapes=[pltpu.VMEM((N_HEADS, 2 * t, D_HEAD), BF16),
                        pltpu.VMEM((N_HEADS, 1, 2 * t), F32),
                        pltpu.VMEM((N_HEADS, 1, 2 * t), F32),
                        pltpu.VMEM((N_HEADS, D_HEAD, 2 * t), F32),
                        pltpu.VMEM((2, t, 2 * t), F32),
                        pltpu.VMEM((2, t, 2 * t), BF16),
                        pltpu.VMEM((N_HEADS, 1, 2 * t), F32)],
        compiler_params=_params(),
        name="prompt_attention",
    )(qkv, qkv, qkv, band, lambda_qk, norm_a_col)


DECODE_PAGES_PER_STEP = 32


def _rows_by_head(x_row):
    pieces = [x_row[:, h * D_HEAD:(h + 1) * D_HEAD] for h in range(N_HEADS)]
    return jnp.concatenate(pieces + pieces, axis=0)


def _decode_kernel(pt_ref, ck_hbm, cv_hbm, dec_ref, rb_ref, lq_ref, na_ref, q_ref, ks_ref, vs_ref, o_ref,
                   kbuf, vbuf, sems, qrows_sc, m_sc, l_sc, acc_sc, *, layer, n_pages_step, lam_init):
    b = pl.program_id(0)
    g = pl.program_id(1)
    n_b = pl.num_programs(0)
    n_groups = pl.num_programs(1)
    step = b * n_groups + g
    slot = lax.rem(step, 2)

    def page_copies(bb, gg, sl):
        copies = []
        for r in range(n_pages_step):
            page = pt_ref[bb, gg * n_pages_step + r]
            copies.append(pltpu.make_async_copy(ck_hbm.at[layer, page], kbuf.at[sl, r], sems.at[sl, 0]))
            copies.append(pltpu.make_async_copy(cv_hbm.at[layer, page], vbuf.at[sl, r], sems.at[sl, 1]))
        return copies

    @pl.when(step == 0)
    def _():
        for cp in page_copies(b, g, slot):
            cp.start()

    wraps = g == n_groups - 1
    next_b = jnp.where(wraps, b + 1, b)
    next_g = jnp.where(wraps, 0, g + 1)

    @pl.when(step + 1 < n_b * n_groups)
    def _():
        for cp in page_copies(next_b, next_g, 1 - slot):
            cp.start()

    n_rows = 2 * N_HEADS
    row = lax.broadcasted_iota(jnp.int32, (n_rows, D_HEAD), 0)
    lane = lax.broadcasted_iota(jnp.int32, (n_rows, D_HEAD), 1)
    own_map = (lane >> 6) == (row >> 2)

    @pl.when(g == 0)
    def _():
        q8 = _rows_by_head(q_ref[...] * (DH_A ** -0.5))
        qrows_sc[...] = jnp.where(own_map, q8, 0.0)
        m_sc[...] = jnp.full(m_sc.shape, NEG, F32)
        l_sc[...] = jnp.zeros(l_sc.shape, F32)
        acc_sc[...] = jnp.zeros(acc_sc.shape, F32)

    qrows = qrows_sc[...]
    qb = qrows.astype(BF16)

    for cp in page_copies(b, g, slot):
        cp.wait()
    k_pages = [kbuf[slot, r] for r in range(n_pages_step)]
    v_pages = [vbuf[slot, r] for r in range(n_pages_step)]

    s = jnp.concatenate([_dot_nt(qb, k_pages[r].astype(BF16)) for r in range(n_pages_step)], axis=1)
    s = s + dec_ref[...]
    m_prev = m_sc[...]
    m_new = jnp.maximum(m_prev, jnp.max(s, axis=-1, keepdims=True))
    alpha = jnp.exp(m_prev - m_new)
    p = jnp.exp(s - m_new)
    l_sc[...] = alpha * l_sc[...] + jnp.sum(p, axis=-1, keepdims=True)
    pb = p.astype(BF16)
    pv = None
    for r in range(n_pages_step):
        part = _dot(pb[:, r * PAGE_ROWS:(r + 1) * PAGE_ROWS], v_pages[r].astype(BF16))
        pv = part if pv is None else pv + part
    acc_sc[...] = alpha * acc_sc[...] + pv
    m_sc[...] = m_new

    @pl.when(g == pl.num_programs(1) - 1)
    def _():
        head = lax.broadcasted_iota(jnp.int32, (n_rows, 1), 0) & (N_HEADS - 1)
        b0 = jnp.zeros((n_rows, 1), F32)
        for h in range(N_HEADS):
            b0 = jnp.where(head == h, rb_ref[0, h], b0)
        s_own = jnp.sum(qrows * _rows_by_head(ks_ref[...]), axis=-1, keepdims=True) + b0
        m_prev = m_sc[...]
        m_fin = jnp.maximum(m_prev, s_own)
        alpha = jnp.exp(m_prev - m_fin)
        p_own = jnp.exp(s_own - m_fin)
        l_fin = alpha * l_sc[...] + p_own
        acc = alpha * acc_sc[...] + p_own * _rows_by_head(vs_ref[...])
        o = acc / l_fin
        lam = _lambda(lq_ref, lam_init)
        attn = o[0:N_HEADS] - lam * o[N_HEADS:n_rows]
        out = _rms(attn, na_ref[...]) * (1.0 - lam_init)
        for h in range(N_HEADS):
            o_ref[:, h * D_HEAD:(h + 1) * D_HEAD] = out[h:h + 1].astype(o_ref.dtype)


def _sample_attention(p_s, cache_k, cache_v, layer, page_table, dec_bias, rel_bias, lambda_qk, norm_a,
                      lam_init, n_pages_step):
    db = p_s.shape[0]
    n_pages = page_table.shape[1]
    n_groups = n_pages // n_pages_step
    gw = n_pages_step * PAGE_ROWS
    p3 = p_s.reshape(db, 1, N_MAIN)
    dec3 = dec_bias.reshape(n_groups, 8, gw)

    def row_spec(col):
        return pl.BlockSpec((None, 1, W_BRANCH), lambda b, g, pt: (b, 0, col // W_BRANCH))

    const2 = lambda b, g, pt: (0, 0)
    page_buffers = pltpu.VMEM((2, n_pages_step, PAGE_ROWS, D_HEAD), F32)
    grid_spec = pltpu.PrefetchScalarGridSpec(
        num_scalar_prefetch=1,
        grid=(db, n_groups),
        in_specs=[pl.BlockSpec(memory_space=pl.ANY),
                  pl.BlockSpec(memory_space=pl.ANY),
                  pl.BlockSpec((None, 8, gw), lambda b, g, pt: (g, 0, 0)),
                  pl.BlockSpec(memory_space=pltpu.SMEM),
                  pl.BlockSpec((4, DH_A), const2),
                  pl.BlockSpec((1, D_HEAD), const2),
                  row_spec(C_AQ), row_spec(C_AK), row_spec(C_AV)],
        out_specs=pl.BlockSpec((None, 1, W_BRANCH), lambda b, g, pt: (b, 0, 0)),
        scratch_shapes=[page_buffers, page_buffers,
                        pltpu.SemaphoreType.DMA((2, 2)),
                        pltpu.VMEM((8, D_HEAD), F32),
                        pltpu.VMEM((8, 1), F32),
                        pltpu.VMEM((8, 1), F32),
                        pltpu.VMEM((8, D_HEAD), F32)])
    out = pl.pallas_call(
        functools.partial(_decode_kernel, layer=layer, n_pages_step=n_pages_step, lam_init=lam_init),
        grid_spec=grid_spec,
        out_shape=jax.ShapeDtypeStruct((db, 1, W_BRANCH), BF16),
        compiler_params=_params(2),
        name="sample_attention",
    )(page_table, cache_k, cache_v, dec3, rel_bias, lambda_qk, norm_a, p3, p3, p3)
    return out.reshape(db, W_BRANCH)


def _log_sigmoid(x):
    return jnp.minimum(x, 0.0) - jnp.log1p(jnp.exp(-jnp.abs(x)))


def _split_bf16(x):
    hi = x.astype(BF16)
    lo = (x - hi.astype(F32)).astype(BF16)
    return hi, lo


def _dot_3pass(a, b):
    a_hi, a_lo = _split_bf16(a)
    b_hi, b_lo = _split_bf16(b)
    return _dot(a_hi, b_hi) + (_dot(a_hi, b_lo) + _dot(a_lo, b_hi))


def _gla_log_decay(bg, wup_ref, bgla_ref):
    x = _dot_3pass(bg, wup_ref[...])
    return _log_sigmoid(x + bgla_ref[...]) / GLA_TAU


def _rotate_half(x):
    width = x.shape[-1]
    lane = lax.broadcasted_iota(jnp.int32, x.shape, x.ndim - 1)
    lo = (lane & (DK - 1)) < DK // 2
    return jnp.where(lo, pltpu.roll(x, width - DK // 2, x.ndim - 1), pltpu.roll(x, DK // 2, x.ndim - 1))


def _rotary(x, cos, sin_signed):
    reps = x.shape[-1] // cos.shape[-1]
    cos = jnp.concatenate([cos] * reps, axis=-1)
    sin_signed = jnp.concatenate([sin_signed] * reps, axis=-1)
    return x * cos + _rotate_half(x) * sin_signed


def _log_gamma(h):
    return math.log(1.0 - 2.0 ** (-5.0 - h))


def _head_of_keys(shape, dim):
    return lax.broadcasted_iota(jnp.int32, shape, dim) >> 6


def _layernorm(x, g):
    xc = x - jnp.mean(x, axis=-1, keepdims=True)
    return xc * lax.rsqrt(jnp.mean(xc * xc, axis=-1, keepdims=True) + EPS) * g


def _block_diag_mask():
    r = lax.broadcasted_iota(jnp.int32, (W_KEYS, W_BRANCH), 0) >> 6
    c = lax.broadcasted_iota(jnp.int32, (W_KEYS, W_BRANCH), 1) >> 7
    return r == c


def _head_blocks(full):
    return jnp.concatenate([full[h * DK:(h + 1) * DK, h * D_HEAD:(h + 1) * D_HEAD]
                            for h in range(N_HEADS)], axis=0)


GROUP = 256


def _linattn_kernel(bq_ref, bk_ref, bv_ref, bg_ref, cq_ref, ck_ref, cv_ref, cos_ref, sin_ref,
                    wup_ref, bgla_ref, nb_ref, nc_ref,
                    bo_ref, co_ref, sg_ref, sr_ref,
                    stg_sc, str_sc, *, rows):
    step = pl.program_id(0)

    @pl.when(step == 0)
    def _():
        stg_sc[...] = jnp.zeros(stg_sc.shape, F32)
        str_sc[...] = jnp.zeros(str_sc.shape, F32)

    n_chunks = rows // CHUNK
    n_groups = rows // GROUP
    cpg = GROUP // CHUNK
    ti = lax.broadcasted_iota(jnp.int32, (GROUP, GROUP), 0)
    si = lax.broadcasted_iota(jnp.int32, (GROUP, GROUP), 1)
    causal = ((ti >> 6) == (si >> 6)) & (ti >= si)
    tril = jnp.where(causal, 1.0, 0.0).astype(BF16)
    dist = (ti - si).astype(F32)
    key_head = _head_of_keys((GROUP, W_KEYS), 1)

    tcol = (lax.broadcasted_iota(jnp.int32, (GROUP, W_KEYS), 0) & (CHUNK - 1)).astype(F32)
    lg_lane = jnp.zeros((GROUP, W_KEYS), F32)
    lg_row = jnp.zeros((W_KEYS, D_HEAD), F32)
    row_head = _head_of_keys((W_KEYS, D_HEAD), 0)
    for h in range(N_HEADS):
        lg_lane = jnp.where(key_head == h, _log_gamma(h), lg_lane)
        lg_row = jnp.where(row_head == h, _log_gamma(h), lg_row)
    g_fwd = jnp.exp((tcol + 1.0) * lg_lane)
    g_rev = jnp.exp((CHUNK - 1.0 - tcol) * lg_lane)
    g_end = jnp.exp(CHUNK * lg_row)

    glog = _gla_log_decay(bg_ref[...], wup_ref, bgla_ref)
    cq_all = _rotary(cq_ref[...], cos_ref[...], sin_ref[...])
    ck_all = _rotary(ck_ref[...], cos_ref[...], sin_ref[...]) * (DK ** -0.5)
    bq_all = bq_ref[...] * (DK ** -0.5)

    g_qd, g_oi, g_upd, g_end_rows = [], [], [], []
    r_qf, r_oi, r_upd = [], [], []
    for g in range(n_groups):
        rs = slice(g * GROUP, (g + 1) * GROUP)
        g_hi, g_lo = _split_bf16(glog[rs])
        gcum = _dot(tril, g_hi) + _dot(tril, g_lo)
        ends = [gcum[(c + 1) * CHUNK - 1:(c + 1) * CHUNK] for c in range(cpg)]
        g_end_rows += ends
        gend_b = jnp.concatenate([jnp.broadcast_to(e, (CHUNK, W_KEYS)) for e in ends], axis=0)
        qd = bq_all[rs] * jnp.exp(gcum)
        bk = bk_ref[rs, :]
        kinv = (bk * jnp.exp(-gcum)).astype(BF16)
        krem = (bk * jnp.exp(gend_b - gcum)).astype(BF16)
        v = bv_ref[rs, :].astype(BF16)
        parts = []
        for h in range(N_HEADS):
            qm = jnp.where(key_head == h, qd, 0.0).astype(BF16)
            sc = jnp.where(causal, _dot_nt(qm, kinv), 0.0)
            parts.append(_dot(sc.astype(BF16), v[:, h * D_HEAD:(h + 1) * D_HEAD]))
        oi = jnp.concatenate(parts, axis=1)
        qdb = qd.astype(BF16)
        for c in range(cpg):
            cs = slice(c * CHUNK, (c + 1) * CHUNK)
            g_qd.append(qdb[cs])
            g_oi.append(oi[cs])
            g_upd.append(_head_blocks(_dot_tn(krem[cs], v[cs])))
        q = cq_all[rs]
        k = ck_all[rs]
        v = cv_ref[rs, :].astype(BF16)
        kb = k.astype(BF16)
        parts = []
        for h in range(N_HEADS):
            qm = jnp.where(key_head == h, q, 0.0).astype(BF16)
            decay = jnp.where(causal, jnp.exp(dist * _log_gamma(h)), 0.0)
            sc = _dot_nt(qm, kb) * decay
            parts.append(_dot(sc.astype(BF16), v[:, h * D_HEAD:(h + 1) * D_HEAD]))
        oi = jnp.concatenate(parts, axis=1)
        krev = (k * g_rev).astype(BF16)
        qfb = (q * g_fwd).astype(BF16)
        for c in range(cpg):
            cs = slice(c * CHUNK, (c + 1) * CHUNK)
            r_qf.append(qfb[cs])
            r_oi.append(oi[cs])
            r_upd.append(_head_blocks(_dot_tn(krev[cs], v[cs])))

    e_cols = jnp.exp(jnp.transpose(jnp.concatenate(g_end_rows, axis=0)))

    bd = _block_diag_mask()

    def state_products(qs, st):
        stb = st.astype(BF16)
        st_bd = jnp.where(bd, jnp.concatenate([stb] * N_HEADS, axis=1), jnp.zeros((), BF16))
        return _dot(qs, st_bd)

    st = stg_sc[...]
    for c in range(n_chunks):
        rs = slice(c * CHUNK, (c + 1) * CHUNK)
        o = g_oi[c] + state_products(g_qd[c], st)
        st = e_cols[:, c:c + 1] * st + g_upd[c]
        for h in range(N_HEADS):
            hs = slice(h * D_HEAD, (h + 1) * D_HEAD)
            bo_ref[rs, hs] = _rms(o[:, hs], nb_ref[:, hs]).astype(bo_ref.dtype)
    stg_sc[...] = st
    st = str_sc[...]
    for c in range(n_chunks):
        rs = slice(c * CHUNK, (c + 1) * CHUNK)
        o = r_oi[c] + state_products(r_qf[c], st)
        st = g_end * st + r_upd[c]
        for h in range(N_HEADS):
            hs = slice(h * D_HEAD, (h + 1) * D_HEAD)
            co_ref[rs, hs] = _layernorm(o[:, hs], nc_ref[:, hs]).astype(co_ref.dtype)
    str_sc[...] = st

    @pl.when(step == pl.num_programs(0) - 1)
    def _():
        for h in range(N_HEADS):
            sg_ref[h] = stg_sc[h * DK:(h + 1) * DK, :]
            sr_ref[h] = str_sc[h * DK:(h + 1) * DK, :]


def _prompt_linear_attention(p, cos, sin, wup, bgla, norm_b, norm_c):
    seq = p.shape[0]
    rows = min(512, seq)

    def col(c, w):
        return pl.BlockSpec((rows, w), lambda i: (i, c // w))

    const = lambda i: (0, 0)
    state = jax.ShapeDtypeStruct((N_HEADS, DK, D_HEAD), F32)
    state_spec = pl.BlockSpec((N_HEADS, DK, D_HEAD), lambda i: (0, 0, 0))
    return pl.pallas_call(
        functools.partial(_linattn_kernel, rows=rows),
        grid=(seq // rows,),
        in_specs=[col(C_BQ, W_KEYS), col(C_BK, W_KEYS), col(C_BV, W_BRANCH), col(C_BG, LANES),
                  col(C_CQ, W_KEYS), col(C_CK, W_KEYS), col(C_CV, W_BRANCH),
                  pl.BlockSpec((rows, LANES), lambda i: (i, 0)),
                  pl.BlockSpec((rows, LANES), lambda i: (i, 0)),
                  pl.BlockSpec((LANES, W_KEYS), const),
                  pl.BlockSpec((1, W_KEYS), const),
                  pl.BlockSpec((1, W_BRANCH), const),
                  pl.BlockSpec((1, W_BRANCH), const)],
        out_specs=[pl.BlockSpec((rows, W_BRANCH), lambda i: (i, 0)),
                   pl.BlockSpec((rows, W_BRANCH), lambda i: (i, 0)),
                   state_spec, state_spec],
        out_shape=[jax.ShapeDtypeStruct((seq, W_BRANCH), BF16),
                   jax.ShapeDtypeStruct((seq, W_BRANCH), BF16),
                   state, state],
        scratch_shapes=[pltpu.VMEM((W_KEYS, D_HEAD), F32),
                        pltpu.VMEM((W_KEYS, D_HEAD), F32)],
        compiler_params=_params(),
        name="prompt_linear_attention",
    )(p, p, p, p, p, p, p, cos, sin, wup, bgla, norm_b, norm_c)


def _to_column(x_row):
    n = x_row.shape[-1]
    r = lax.broadcasted_iota(jnp.int32, (n, n), 0)
    c = lax.broadcasted_iota(jnp.int32, (n, n), 1)
    return jnp.sum(jnp.where(r == c, jnp.broadcast_to(x_row, (n, n)), 0.0), axis=-1, keepdims=True)


def _recurrent_kernel(bq_ref, bk_ref, bv_ref, bg_ref, cq_ref, ck_ref, cv_ref, cos_ref, sin_ref,
                      wup_ref, bgla_ref, nb_ref, nc_ref, sg0_ref, sr0_ref,
                      bo_ref, co_ref, sg_ref, sr_ref):
    decay = _to_column(jnp.exp(_gla_log_decay(bg_ref[...], wup_ref, bgla_ref)))
    bq = _to_column(bq_ref[...] * (DK ** -0.5))
    bk = _to_column(bk_ref[...])
    cq = _to_column(_rotary(cq_ref[...], cos_ref[...], sin_ref[...]))
    ck = _to_column(_rotary(ck_ref[...], cos_ref[...], sin_ref[...]) * (DK ** -0.5))
    for h in range(N_HEADS):
        ks = slice(h * DK, (h + 1) * DK)
        hs = slice(h * D_HEAD, (h + 1) * D_HEAD)
        s_new = decay[ks] * sg0_ref[h] + bk[ks] * bv_ref[:, hs]
        sg_ref[h] = s_new
        o = jnp.sum(bq[ks] * s_new, axis=0, keepdims=True)
        bo_ref[:, hs] = _rms(o, nb_ref[:, hs]).astype(bo_ref.dtype)
        s_new = math.exp(_log_gamma(h)) * sr0_ref[h] + ck[ks] * cv_ref[:, hs]
        sr_ref[h] = s_new
        o = jnp.sum(cq[ks] * s_new, axis=0, keepdims=True)
        co_ref[:, hs] = _layernorm(o, nc_ref[:, hs]).astype(co_ref.dtype)


def _sample_linear_attention(p_s, cos, sin, wup, bgla, norm_b, norm_c, state_gla, state_ret, layer):
    db = p_s.shape[0]
    p3 = p_s.reshape(db, 1, N_MAIN)

    def col(c, w):
        return pl.BlockSpec((None, 1, w), lambda b: (b, 0, c // w))

    const = lambda b: (0, 0)
    state_spec = pl.BlockSpec((None, N_HEADS, DK, D_HEAD), lambda b: (b, 0, 0, 0))
    state_in_spec = pl.BlockSpec((None, None, N_HEADS, DK, D_HEAD), lambda b: (layer, b, 0, 0, 0))
    out_row = pl.BlockSpec((None, 1, W_BRANCH), lambda b: (b, 0, 0))
    state = jax.ShapeDtypeStruct((db, N_HEADS, DK, D_HEAD), F32)
    bo, co, sg, sr = pl.pallas_call(
        _recurrent_kernel,
        grid=(db,),
        in_specs=[col(C_BQ, W_KEYS), col(C_BK, W_KEYS), col(C_BV, W_BRANCH), col(C_BG, LANES),
                  col(C_CQ, W_KEYS), col(C_CK, W_KEYS), col(C_CV, W_BRANCH),
                  pl.BlockSpec((1, LANES), const), pl.BlockSpec((1, LANES), const),
                  pl.BlockSpec((LANES, W_KEYS), const),
                  pl.BlockSpec((1, W_KEYS), const),
                  pl.BlockSpec((1, W_BRANCH), const),
                  pl.BlockSpec((1, W_BRANCH), const),
                  state_in_spec, state_in_spec],
        out_specs=[out_row, out_row, state_spec, state_spec],
        out_shape=[jax.ShapeDtypeStruct((db, 1, W_BRANCH), BF16),
                   jax.ShapeDtypeStruct((db, 1, W_BRANCH), BF16),
                   state, state],
        compiler_params=_params(),
        name="sample_linear_attention",
    )(p3, p3, p3, p3, p3, p3, p3, cos, sin, wup, bgla, norm_b, norm_c, state_gla, state_ret)
    return bo.reshape(db, W_BRANCH), co.reshape(db, W_BRANCH), sg, sr


def _merge_kernel(x_ref, aot_ref, bo_ref, co_ref, br_ref, cg_ref, gates_ref,
                  wa_ref, wb_ref, wc_ref, wo_ref, o_ref):
    ya = _dot_tn(aot_ref[...], wa_ref[...])
    br = br_ref[...].astype(F32)
    yb = _dot((bo_ref[...].astype(F32) * (br * _sigmoid(br))).astype(BF16), wb_ref[...])
    cg = cg_ref[...].astype(F32)
    yc = _dot((co_ref[...].astype(F32) * (cg * _sigmoid(cg))).astype(BF16), wc_ref[...])
    mix = (_sigmoid(gates_ref[:, 0:D_MODEL].astype(F32)) * ya
           + _sigmoid(gates_ref[:, D_MODEL:2 * D_MODEL].astype(F32)) * yb
           + _sigmoid(gates_ref[:, 2 * D_MODEL:3 * D_MODEL].astype(F32)) * yc)
    o_ref[...] = x_ref[...] + _dot(mix.astype(BF16), wo_ref[...])


def _merge(x, aot, bo, co, p, wa, wb, wc, wo, layer):
    m = x.shape[0]
    tm = min(512, m)
    row = lambda i: (i, 0)
    of_layer = lambda i: (layer, 0, 0)
    return pl.pallas_call(
        _merge_kernel,
        grid=(m // tm,),
        in_specs=[pl.BlockSpec((tm, D_MODEL), row),
                  pl.BlockSpec((W_BRANCH, tm), lambda i: (0, i)),
                  pl.BlockSpec((tm, W_BRANCH), row),
                  pl.BlockSpec((tm, W_BRANCH), row),
                  pl.BlockSpec((tm, W_BRANCH), lambda i: (i, C_BR // W_BRANCH)),
                  pl.BlockSpec((tm, W_BRANCH), lambda i: (i, C_CG // W_BRANCH)),
                  pl.BlockSpec((tm, 3 * D_MODEL), lambda i: (i, C_GATES // (3 * D_MODEL))),
                  _resident((None, W_BRANCH, D_MODEL), of_layer),
                  _resident((None, W_BRANCH, D_MODEL), of_layer),
                  _resident((None, W_BRANCH, D_MODEL), of_layer),
                  _resident((None, D_MODEL, D_MODEL), of_layer)],
        out_specs=pl.BlockSpec((tm, D_MODEL), row),
        out_shape=jax.ShapeDtypeStruct((m, D_MODEL), F32),
        compiler_params=_params(),
        name="merge",
    )(x, aot, bo, co, p, p, p, wa, wb, wc, wo)


def _rope_tables(pos):
    half = DK // 2
    inv = ROPE_BASE ** (-jnp.arange(half, dtype=F32) / half)
    ang = pos.astype(F32)[:, None] * inv[None, :]
    cos = jnp.cos(ang)
    sin = jnp.sin(ang)
    cos_t = jnp.tile(jnp.concatenate([cos, cos], axis=-1), (1, LANES // DK))
    sin_t = jnp.tile(jnp.concatenate([-sin, sin], axis=-1), (1, LANES // DK))
    return cos_t, sin_t


def _reorder_w_in(w):
    s_a = 3 * W_BRANCH
    s_b = s_a + 2 * W_KEYS + 2 * W_BRANCH
    s_g = s_b + GLA_RANK
    s_c = s_g + 2 * W_KEYS + 2 * W_BRANCH
    s_br = s_b - W_BRANCH
    s_cg = s_c - W_BRANCH
    pad = jnp.zeros(w.shape[:-1] + (N_MAIN - C_BG - GLA_RANK,), w.dtype)
    return jnp.concatenate([w[..., s_c:], w[..., s_br:s_b], w[..., s_cg:s_c],
                            w[..., :s_br], w[..., s_g:s_cg], w[..., s_b:s_g], pad], axis=-1)


def kernel(x_prompt, x_sample, cache_k, cache_v, state_gla, state_ret, page_table, w_in, w_gla_up, b_gla, lambda_qk, rel_bias, norm_a, norm_b, norm_c, w_a, w_b, w_c, w_out, norm_ffn1, ffn1_wg, ffn1_wu, ffn1_wd, norm_mix, norm_ffn2, ffn2_wg, ffn2_wu, ffn2_wd, norm_final):
    batch, seq, _ = x_prompt.shape
    db, dec_seq, _ = x_sample.shape
    depth = w_in.shape[0]
    assert batch == 1 and dec_seq == 1
    n_pages = page_table.shape[1]
    past = n_pages * PAGE_SIZE
    n_pool = cache_k.shape[1]
    n_pages_step = math.gcd(DECODE_PAGES_PER_STEP, n_pages)

    band, dec_bias = _bias_tables(rel_bias, past, n_pages_step * PAGE_SIZE)
    cos_p, sin_p = _rope_tables(jnp.arange(seq))
    cos_s, sin_s = _rope_tables(past + jnp.arange(dec_seq))
    ck4 = cache_k.reshape(depth, n_pool, PAGE_ROWS, D_HEAD)
    cv4 = cache_v.reshape(depth, n_pool, PAGE_ROWS, D_HEAD)
    g_final = norm_final.reshape(1, D_MODEL)

    xp = x_prompt.reshape(seq, D_MODEL)
    xs = x_sample.reshape(db, D_MODEL)
    w_in_b = _reorder_w_in(w_in).astype(BF16)
    ffn1_w = tuple(w.astype(BF16) for w in (ffn1_wg, ffn1_wu, ffn1_wd))
    ffn2_w = tuple(w.astype(BF16) for w in (ffn2_wg, ffn2_wu, ffn2_wd))
    proj_w = tuple(w.astype(BF16) for w in (w_a, w_b, w_c, w_out))
    outs = {k: [] for k in ("kp", "vp", "gp", "rp", "ks", "vs", "gs", "rs")}
    for l in range(depth):
        lam_init = 0.8 - 0.6 * math.exp(-0.3 * l)
        last = l == depth - 1
        wup = jnp.concatenate([w_gla_up[l], jnp.zeros((LANES - GLA_RANK, W_KEYS), F32)], axis=0)
        bgla = b_gla[l].reshape(1, W_KEYS)
        nb = norm_b[l].reshape(1, W_BRANCH)
        nc = norm_c[l].reshape(1, W_BRANCH)
        na = norm_a[l].reshape(1, D_HEAD)
        na_col = norm_a[l].reshape(D_HEAD, 1)
        ffn1 = (norm_ffn1[l].reshape(1, D_MODEL), *ffn1_w, l)
        ffn2 = (norm_ffn2[l].reshape(1, D_MODEL), *ffn2_w, l)
        g_mix = norm_mix[l].reshape(1, D_MODEL)

        xp = _ffn_half(xp, *ffn1, g_final, False)
        p, gate_cols, qkv, k_new, v_new = _inproj(xp, g_mix, w_in_b, l)
        aot = _prompt_attention(qkv, band, lambda_qk[l], na_col, lam_init)
        bo, co, sg, sr = _prompt_linear_attention(p, cos_p, sin_p, wup, bgla, nb, nc)
        xp = _merge(xp, aot, bo, co, gate_cols, *proj_w, l)
        xp = _ffn_half(xp, *ffn2, g_final, last)
        outs["kp"].append(k_new[None])
        outs["vp"].append(v_new[None])
        outs["gp"].append(sg[None])
        outs["rp"].append(sr[None])

        xs = _ffn_half(xs, *ffn1, g_final, False)
        p_s, gate_cols, _, k_new, v_new = _inproj(xs, g_mix, w_in_b, l)
        ao = _sample_attention(p_s, ck4, cv4, l, page_table, dec_bias, rel_bias, lambda_qk[l], na,
                               lam_init, n_pages_step)
        bo, co, sg, sr = _sample_linear_attention(p_s, cos_s, sin_s, wup, bgla, nb, nc,
                                                  state_gla, state_ret, l)
        xs = _merge(xs, ao.T, bo, co, gate_cols, *proj_w, l)
        xs = _ffn_half(xs, *ffn2, g_final, last)
        outs["ks"].append(k_new[:, None])
        outs["vs"].append(v_new[:, None])
        outs["gs"].append(sg)
        outs["rs"].append(sr)

    st = lambda key: jnp.stack(outs[key])
    return (xp.reshape(batch, seq, D_MODEL), xs.reshape(db, dec_seq, D_MODEL),
            st("kp"), st("vp"), st("gp"), st("rp"), st("ks"), st("vs"), st("gs"), st("rs"))
```

```python
import functools
import math

import jax
import jax.numpy as jnp
from jax import lax
from jax.experimental import pallas as pl
from jax.experimental.pallas import tpu as pltpu

F32 = jnp.float32
BF16 = jnp.bfloat16

D_MODEL = 1024
N_HEADS = 4
DH_A = 64
D_HEAD = 128
DK = 64
GLA_RANK = 16
GLA_TAU = 16.0
ROPE_BASE = 10000.0
NUM_BUCKETS = 32
MAX_DISTANCE = 128
D_FF = 2816
PAGE_SIZE = 128
CHUNK = 64
EPS = 1e-6
NEG = -1e30
LOG2E = math.log2(math.e)
W_BRANCH = N_HEADS * D_HEAD
W_KEYS = N_HEADS * DK
PAGE_ROWS = PAGE_SIZE * N_HEADS

LANES = 128
VMEM_LIMIT_BYTES = 56 * 1024 * 1024

C_GATES = 0
C_BR = 3072
C_CG = 3584
N_GATE = 4096
C_AQ = 0
C_AK = 512
C_AV = 1024
C_BQ = 1536
C_BK = 1792
C_BV = 2048
C_CQ = 2560
C_CK = 2816
C_CV = 3072
C_BG = 3584
N_MAIN = 3712
N_PROJ = N_GATE + N_MAIN

FAR_DISTANCE = 113


def _params(n_axes=1, vmem=VMEM_LIMIT_BYTES):
    return pltpu.CompilerParams(dimension_semantics=("arbitrary",) * n_axes,
                                vmem_limit_bytes=vmem)


def _resident(shape, index_map):
    return pl.BlockSpec(shape, index_map, pipeline_mode=pl.Buffered(1))


def _rms(x, g):
    return x * lax.rsqrt(jnp.mean(x * x, axis=-1, keepdims=True) + EPS) * g


def _sigmoid(x):
    return 1.0 / (1.0 + jnp.exp(-x))


def _dot(a, b):
    return jnp.dot(a, b, preferred_element_type=F32)


def _dot_nt(a, b):
    return lax.dot_general(a, b, (((1,), (1,)), ((), ())), preferred_element_type=F32)


def _dot_tn(a, b):
    return lax.dot_general(a, b, (((0,), (0,)), ((), ())), preferred_element_type=F32)


FF_CHUNK = 1408


def _ffn_kernel(x_ref, g_ref, wg_ref, wu_ref, wd_ref, gf_ref, o_ref, *, final_norm):
    x = x_ref[...]
    n = _rms(x, g_ref[...]).astype(BF16)
    acc = None
    for c in range(0, D_FF, FF_CHUNK):
        a = _dot(n, wg_ref[:, c:c + FF_CHUNK])
        u = _dot(n, wu_ref[:, c:c + FF_CHUNK])
        h = (a * _sigmoid(a) * u).astype(BF16)
        part = _dot(h, wd_ref[c:c + FF_CHUNK, :])
        acc = part if acc is None else acc + part
    y = x + 0.5 * acc
    if final_norm:
        y = _rms(y, gf_ref[...])
    o_ref[...] = y


def _ffn_half(x, g, wg, wu, wd, layer, g_final, final_norm):
    m = x.shape[0]
    tm = min(512, m)
    row = lambda i: (i, 0)
    const = lambda i: (0, 0)
    of_layer = lambda i: (layer, 0, 0)
    return pl.pallas_call(
        functools.partial(_ffn_kernel, final_norm=final_norm),
        grid=(m // tm,),
        in_specs=[pl.BlockSpec((tm, D_MODEL), row),
                  _resident((1, D_MODEL), const),
                  _resident((None, D_MODEL, D_FF), of_layer),
                  _resident((None, D_MODEL, D_FF), of_layer),
                  _resident((None, D_FF, D_MODEL), of_layer),
                  _resident((1, D_MODEL), const)],
        out_specs=pl.BlockSpec((tm, D_MODEL), row),
        out_shape=jax.ShapeDtypeStruct((m, D_MODEL), F32),
        compiler_params=_params(),
        name="ffn_half",
    )(x, g, wg, wu, wd, g_final)


PROJ_CHUNK = 1024


def _inproj_kernel(x_ref, g_ref, w_ref, p_ref, gate_ref, qkv_ref, k_ref, v_ref):
    n = _rms(x_ref[...], g_ref[...]).astype(BF16)
    for c in range(0, N_GATE, PROJ_CHUNK):
        gate_ref[:, c:c + PROJ_CHUNK] = _dot(n, w_ref[:, c:c + PROJ_CHUNK]).astype(BF16)
    for c in range(0, N_MAIN, PROJ_CHUNK):
        w = min(PROJ_CHUNK, N_MAIN - c)
        p_ref[:, c:c + w] = _dot(n, w_ref[:, N_GATE + c:N_GATE + c + w])
    qkv_ref[:, 0:W_BRANCH] = (p_ref[:, C_AQ:C_AQ + W_BRANCH] * (DH_A ** -0.5 * LOG2E)).astype(BF16)
    qkv_ref[:, W_BRANCH:3 * W_BRANCH] = p_ref[:, C_AK:C_AK + 2 * W_BRANCH].astype(BF16)
    for h in range(N_HEADS):
        k_ref[:, h, :] = p_ref[:, C_AK + h * D_HEAD:C_AK + (h + 1) * D_HEAD]
        v_ref[:, h, :] = p_ref[:, C_AV + h * D_HEAD:C_AV + (h + 1) * D_HEAD]


def _inproj(x, g, w, layer):
    m = x.shape[0]
    tm = min(256, m)
    row = lambda i: (i, 0)
    const = lambda i: (0, 0)
    rows3 = pl.BlockSpec((tm, N_HEADS, D_HEAD), lambda i: (i, 0, 0))
    return pl.pallas_call(
        _inproj_kernel,
        grid=(m // tm,),
        in_specs=[pl.BlockSpec((tm, D_MODEL), row),
                  _resident((1, D_MODEL), const),
                  _resident((None, D_MODEL, N_PROJ), lambda i: (layer, 0, 0))],
        out_specs=[pl.BlockSpec((tm, N_MAIN), row),
                   pl.BlockSpec((tm, N_GATE), row),
                   pl.BlockSpec((tm, 3 * W_BRANCH), row),
                   rows3, rows3],
        out_shape=[jax.ShapeDtypeStruct((m, N_MAIN), F32),
                   jax.ShapeDtypeStruct((m, N_GATE), BF16),
                   jax.ShapeDtypeStruct((m, 3 * W_BRANCH), BF16),
                   jax.ShapeDtypeStruct((m, N_HEADS, D_HEAD), F32),
                   jax.ShapeDtypeStruct((m, N_HEADS, D_HEAD), F32)],
        compiler_params=_params(),
        name="inproj",
    )(x, g, w)


def _bucket_of(rel):
    n = jnp.maximum(rel, 0)
    max_exact = NUM_BUCKETS // 2
    nf = jnp.maximum(n, max_exact).astype(F32)
    large = max_exact + (jnp.log(nf / max_exact) / math.log(MAX_DISTANCE / max_exact)
                         * (NUM_BUCKETS - max_exact)).astype(jnp.int32)
    large = jnp.minimum(large, NUM_BUCKETS - 1)
    return jnp.where(n < max_exact, n, large)


def _bias_of(bucket, rb_ref, head):
    val = jnp.zeros(bucket.shape, F32)
    for b in range(NUM_BUCKETS):
        val = jnp.where(bucket == b, rb_ref[b, head], val)
    return val


def _bias_kernel(rb_ref, band_ref, dec_ref, *, past, group_width):
    kk = lax.broadcasted_iota(jnp.int32, (LANES, LANES), 0)
    qq = lax.broadcasted_iota(jnp.int32, (LANES, LANES), 1)
    for d in range(2):
        rel = qq - kk + d * LANES
        bucket = _bucket_of(rel)
        for h in range(N_HEADS):
            val = (_bias_of(bucket, rb_ref, h) - rb_ref[NUM_BUCKETS - 1, h]) * LOG2E
            band_ref[d, h] = jnp.where(rel >= 0, val, NEG)
    rows, width = dec_ref.shape
    row = lax.broadcasted_iota(jnp.int32, (rows, width), 0)
    col = lax.broadcasted_iota(jnp.int32, (rows, width), 1)
    rel = past - ((row >> 3) * group_width + (col >> 2))
    bucket = _bucket_of(rel)
    head = col & (N_HEADS - 1)
    own = head == (row & (N_HEADS - 1))
    val = jnp.zeros((rows, width), F32)
    for h in range(N_HEADS):
        val = jnp.where(head == h, _bias_of(bucket, rb_ref, h), val)
    dec_ref[...] = jnp.where(own, val, NEG)


def _bias_tables(rel_bias, past, group_width):
    n_groups = past // group_width
    return pl.pallas_call(
        functools.partial(_bias_kernel, past=past, group_width=group_width),
        in_specs=[pl.BlockSpec(memory_space=pltpu.SMEM)],
        out_shape=[jax.ShapeDtypeStruct((2, N_HEADS, LANES, LANES), F32),
                   jax.ShapeDtypeStruct((n_groups * 8, group_width * N_HEADS), F32)],
        compiler_params=pltpu.CompilerParams(vmem_limit_bytes=VMEM_LIMIT_BYTES),
        name="bias_tables",
    )(rel_bias)


def _lambda(lq_ref, lam_init):
    lq = lq_ref[...]
    a = jnp.sum(lq[0:1] * lq[1:2], axis=-1, keepdims=True)
    b = jnp.sum(lq[2:3] * lq[3:4], axis=-1, keepdims=True)
    return jnp.exp(a) - jnp.exp(b) + lam_init


def _first_map_lanes(rows):
    lane = lax.broadcasted_iota(jnp.int32, (rows, D_HEAD), 1)
    return lane < DH_A


SOFTMAX_STRIP = 2 * LANES


def _attn_kernel(q_ref, k_ref, v_ref, band_ref, lq_ref, na_ref, o_ref,
                 qbd_sc, m_sc, l_sc, acc_sc, s_sc, p_sc, a_sc, sn_sc, pn_sc, *, t, lam_init):
    i = pl.program_id(0)
    nsub = t // LANES
    first_map = _first_map_lanes(t)

    for h in range(N_HEADS):
        q_h = q_ref[:, h * D_HEAD:(h + 1) * D_HEAD]
        zero = jnp.zeros_like(q_h)
        qbd_sc[h, 0:t, :] = jnp.where(first_map, q_h, zero)
        qbd_sc[h, t:2 * t, :] = jnp.where(first_map, zero, q_h)
    m_sc[...] = jnp.full(m_sc.shape, NEG, F32)
    l_sc[...] = jnp.zeros(l_sc.shape, F32)
    acc_sc[...] = jnp.zeros(acc_sc.shape, F32)

    def stage_scores(h, ks, slot, lead_of, s_buf):
        s = _dot_nt(k_ref[pl.ds(ks, t), h * D_HEAD:(h + 1) * D_HEAD], qbd_sc[h])
        if lead_of is not None:
            rows = []
            for kc in range(nsub):
                cols = []
                for mp in range(2):
                    for qr in range(nsub):
                        sub = s[kc * LANES:(kc + 1) * LANES, mp * t + qr * LANES:mp * t + (qr + 1) * LANES]
                        lead = lead_of(kc, qr)
                        if lead < 0:
                            sub = jnp.full_like(sub, NEG)
                        elif lead < 2:
                            sub = sub + band_ref[lead, h]
                        cols.append(sub)
                rows.append(jnp.concatenate(cols, axis=1))
            s = jnp.concatenate(rows, axis=0) if nsub > 1 else rows[0]
        s_buf[slot] = s

    def stage_softmax(h, slot, s_buf, p_buf):
        m_prev = m_sc[h]
        m_new = jnp.maximum(m_prev, jnp.max(s_buf[slot], axis=0, keepdims=True))
        alpha = jnp.exp2(m_prev - m_new)
        sums = []
        for c0 in range(0, 2 * t, SOFTMAX_STRIP):
            cs = slice(c0, c0 + SOFTMAX_STRIP)
            p = jnp.exp2(s_buf[slot, :, cs] - m_new[:, cs])
            sums.append(jnp.sum(p, axis=0, keepdims=True))
            p_buf[slot, :, cs] = p.astype(BF16)
        l_sc[h] = alpha * l_sc[h] + jnp.concatenate(sums, axis=1)
        a_sc[h] = alpha
        m_sc[h] = m_new

    def stage_values(h, ks, slot, p_buf):
        v_h = v_ref[pl.ds(ks, t), h * D_HEAD:(h + 1) * D_HEAD]
        acc_sc[h] = a_sc[h] * acc_sc[h] + _dot_tn(v_h, p_buf[slot])

    def kv_block(ks, lead_of, s_buf, p_buf):
        stage_scores(0, ks, 0, lead_of, s_buf)
        for h in range(N_HEADS):
            if h + 1 < N_HEADS:
                stage_scores(h + 1, ks, (h + 1) % 2, lead_of, s_buf)
            stage_softmax(h, h % 2, s_buf, p_buf)
            stage_values(h, ks, h % 2, p_buf)

    def far_block(j, carry):
        kv_block(pl.multiple_of(j * t, t), None, s_sc, p_sc)
        return carry

    lax.fori_loop(0, jnp.maximum(i - 1, 0), far_block, 0)

    @pl.when(i >= 1)
    def _():
        kv_block(pl.multiple_of((i - 1) * t, t), lambda kc, qr: qr - kc + nsub, sn_sc, pn_sc)

    kv_block(pl.multiple_of(i * t, t), lambda kc, qr: qr - kc, sn_sc, pn_sc)
    lam = _lambda(lq_ref, lam_init)
    for h in range(N_HEADS):
        o = acc_sc[h] / l_sc[h]
        attn = o[:, 0:t] - lam * o[:, t:2 * t]
        norm = lax.rsqrt(jnp.mean(attn * attn, axis=0, keepdims=True) + EPS)
        o_ref[h * D_HEAD:(h + 1) * D_HEAD, :] = (
            attn * norm * na_ref[...] * (1.0 - lam_init)).astype(o_ref.dtype)


def _prompt_attention(qkv, band, lambda_qk, norm_a_col, lam_init):
    seq = qkv.shape[0]
    t = min(512, seq)
    assert t % LANES == 0 and seq % t == 0 and t + 1 >= FAR_DISTANCE
    const2 = lambda i: (0, 0)
    return pl.pallas_call(
        functools.partial(_attn_kernel, t=t, lam_init=lam_init),
        grid=(seq // t,),
        in_specs=[pl.BlockSpec((t, W_BRANCH), lambda i: (i, 0)),
                  _resident((seq, W_BRANCH), lambda i: (0, 1)),
                  _resident((seq, W_BRANCH), lambda i: (0, 2)),
                  _resident((2, N_HEADS, LANES, LANES), lambda i: (0, 0, 0, 0)),
                  _resident((4, DH_A), const2),
                  _resident((D_HEAD, 1), const2)],
        out_specs=pl.BlockSpec((W_BRANCH, t), lambda i: (0, i)),
        out_shape=jax.ShapeDtypeStruct((W_BRANCH, seq), BF16),
        scratch_shapes=[pltpu.VMEM((N_HEADS, 2 * t, D_HEAD), BF16),
                        pltpu.VMEM((N_HEADS, 1, 2 * t), F32),
                        pltpu.VMEM((N_HEADS, 1, 2 * t), F32),
                        pltpu.VMEM((N_HEADS, D_HEAD, 2 * t), F32),
                        pltpu.VMEM((2, t, 2 * t), F32),
                        pltpu.VMEM((2, t, 2 * t), BF16),
                        pltpu.VMEM((N_HEADS, 1, 2 * t), F32),
                        pltpu.VMEM((2, t, 2 * t), F32),
                        pltpu.VMEM((2, t, 2 * t), BF16)],
        compiler_params=_params(),
        name="prompt_attention",
    )(qkv, qkv, qkv, band, lambda_qk, norm_a_col)


DECODE_PAGES_PER_STEP = 32


def _rows_by_head(x_row):
    pieces = [x_row[:, h * D_HEAD:(h + 1) * D_HEAD] for h in range(N_HEADS)]
    return jnp.concatenate(pieces + pieces, axis=0)


def _decode_kernel(pt_ref, ck_hbm, cv_hbm, dec_ref, rb_ref, lq_ref, na_ref, q_ref, ks_ref, vs_ref, o_ref,
                   kbuf, vbuf, sems, qrows_sc, m_sc, l_sc, acc_sc, *, layer, n_pages_step, lam_init):
    b = pl.program_id(0)
    g = pl.program_id(1)
    n_b = pl.num_programs(0)
    n_groups = pl.num_programs(1)
    step = b * n_groups + g
    slot = lax.rem(step, 2)

    def page_copies(bb, gg, sl):
        copies = []
        for r in range(n_pages_step):
            page = pt_ref[bb, gg * n_pages_step + r]
            copies.append(pltpu.make_async_copy(ck_hbm.at[layer, page], kbuf.at[sl, r], sems.at[sl, 0]))
            copies.append(pltpu.make_async_copy(cv_hbm.at[layer, page], vbuf.at[sl, r], sems.at[sl, 1]))
        return copies

    @pl.when(step == 0)
    def _():
        for cp in page_copies(b, g, slot):
            cp.start()

    wraps = g == n_groups - 1
    next_b = jnp.where(wraps, b + 1, b)
    next_g = jnp.where(wraps, 0, g + 1)

    @pl.when(step + 1 < n_b * n_groups)
    def _():
        for cp in page_copies(next_b, next_g, 1 - slot):
            cp.start()

    n_rows = 2 * N_HEADS
    row = lax.broadcasted_iota(jnp.int32, (n_rows, D_HEAD), 0)
    lane = lax.broadcasted_iota(jnp.int32, (n_rows, D_HEAD), 1)
    own_map = (lane >> 6) == (row >> 2)

    @pl.when(g == 0)
    def _():
        q8 = _rows_by_head(q_ref[...] * (DH_A ** -0.5))
        qrows_sc[...] = jnp.where(own_map, q8, 0.0)
        m_sc[...] = jnp.full(m_sc.shape, NEG, F32)
        l_sc[...] = jnp.zeros(l_sc.shape, F32)
        acc_sc[...] = jnp.zeros(acc_sc.shape, F32)

    qrows = qrows_sc[...]
    qb = qrows.astype(BF16)

    for cp in page_copies(b, g, slot):
        cp.wait()
    k_pages = [kbuf[slot, r] for r in range(n_pages_step)]
    v_pages = [vbuf[slot, r] for r in range(n_pages_step)]

    s = jnp.concatenate([_dot_nt(qb, k_pages[r].astype(BF16)) for r in range(n_pages_step)], axis=1)
    s = s + dec_ref[...]
    m_prev = m_sc[...]
    m_new = jnp.maximum(m_prev, jnp.max(s, axis=-1, keepdims=True))
    alpha = jnp.exp(m_prev - m_new)
    p = jnp.exp(s - m_new)
    l_sc[...] = alpha * l_sc[...] + jnp.sum(p, axis=-1, keepdims=True)
    pb = p.astype(BF16)
    pv = None
    for r in range(n_pages_step):
        part = _dot(pb[:, r * PAGE_ROWS:(r + 1) * PAGE_ROWS], v_pages[r].astype(BF16))
        pv = part if pv is None else pv + part
    acc_sc[...] = alpha * acc_sc[...] + pv
    m_sc[...] = m_new

    @pl.when(g == pl.num_programs(1) - 1)
    def _():
        head = lax.broadcasted_iota(jnp.int32, (n_rows, 1), 0) & (N_HEADS - 1)
        b0 = jnp.zeros((n_rows, 1), F32)
        for h in range(N_HEADS):
            b0 = jnp.where(head == h, rb_ref[0, h], b0)
        s_own = jnp.sum(qrows * _rows_by_head(ks_ref[...]), axis=-1, keepdims=True) + b0
        m_prev = m_sc[...]
        m_fin = jnp.maximum(m_prev, s_own)
        alpha = jnp.exp(m_prev - m_fin)
        p_own = jnp.exp(s_own - m_fin)
        l_fin = alpha * l_sc[...] + p_own
        acc = alpha * acc_sc[...] + p_own * _rows_by_head(vs_ref[...])
        o = acc / l_fin
        lam = _lambda(lq_ref, lam_init)
        attn = o[0:N_HEADS] - lam * o[N_HEADS:n_rows]
        out = _rms(attn, na_ref[...]) * (1.0 - lam_init)
        for h in range(N_HEADS):
            o_ref[:, h * D_HEAD:(h + 1) * D_HEAD] = out[h:h + 1].astype(o_ref.dtype)


def _sample_attention(p_s, cache_k, cache_v, layer, page_table, dec_bias, rel_bias, lambda_qk, norm_a,
                      lam_init, n_pages_step):
    db = p_s.shape[0]
    n_pages = page_table.shape[1]
    n_groups = n_pages // n_pages_step
    gw = n_pages_step * PAGE_ROWS
    p3 = p_s.reshape(db, 1, N_MAIN)
    dec3 = dec_bias.reshape(n_groups, 8, gw)

    def row_spec(col):
        return pl.BlockSpec((None, 1, W_BRANCH), lambda b, g, pt: (b, 0, col // W_BRANCH))

    const2 = lambda b, g, pt: (0, 0)
    page_buffers = pltpu.VMEM((2, n_pages_step, PAGE_ROWS, D_HEAD), F32)
    grid_spec = pltpu.PrefetchScalarGridSpec(
        num_scalar_prefetch=1,
        grid=(db, n_groups),
        in_specs=[pl.BlockSpec(memory_space=pl.ANY),
                  pl.BlockSpec(memory_space=pl.ANY),
                  pl.BlockSpec((None, 8, gw), lambda b, g, pt: (g, 0, 0)),
                  pl.BlockSpec(memory_space=pltpu.SMEM),
                  pl.BlockSpec((4, DH_A), const2),
                  pl.BlockSpec((1, D_HEAD), const2),
                  row_spec(C_AQ), row_spec(C_AK), row_spec(C_AV)],
        out_specs=pl.BlockSpec((None, 1, W_BRANCH), lambda b, g, pt: (b, 0, 0)),
        scratch_shapes=[page_buffers, page_buffers,
                        pltpu.SemaphoreType.DMA((2, 2)),
                        pltpu.VMEM((8, D_HEAD), F32),
                        pltpu.VMEM((8, 1), F32),
                        pltpu.VMEM((8, 1), F32),
                        pltpu.VMEM((8, D_HEAD), F32)])
    out = pl.pallas_call(
        functools.partial(_decode_kernel, layer=layer, n_pages_step=n_pages_step, lam_init=lam_init),
        grid_spec=grid_spec,
        out_shape=jax.ShapeDtypeStruct((db, 1, W_BRANCH), BF16),
        compiler_params=_params(2),
        name="sample_attention",
    )(page_table, cache_k, cache_v, dec3, rel_bias, lambda_qk, norm_a, p3, p3, p3)
    return out.reshape(db, W_BRANCH)


def _log_sigmoid(x):
    return jnp.minimum(x, 0.0) - jnp.log1p(jnp.exp(-jnp.abs(x)))


def _split_bf16(x):
    hi = x.astype(BF16)
    lo = (x - hi.astype(F32)).astype(BF16)
    return hi, lo


def _dot_3pass(a, b):
    a_hi, a_lo = _split_bf16(a)
    b_hi, b_lo = _split_bf16(b)
    return _dot(a_hi, b_hi) + (_dot(a_hi, b_lo) + _dot(a_lo, b_hi))


def _gla_log_decay(bg, wup_ref, bgla_ref):
    x = _dot_3pass(bg, wup_ref[...])
    return _log_sigmoid(x + bgla_ref[...]) / GLA_TAU


def _rotate_half(x):
    width = x.shape[-1]
    lane = lax.broadcasted_iota(jnp.int32, x.shape, x.ndim - 1)
    lo = (lane & (DK - 1)) < DK // 2
    return jnp.where(lo, pltpu.roll(x, width - DK // 2, x.ndim - 1), pltpu.roll(x, DK // 2, x.ndim - 1))


def _rotary(x, cos, sin_signed):
    reps = x.shape[-1] // cos.shape[-1]
    cos = jnp.concatenate([cos] * reps, axis=-1)
    sin_signed = jnp.concatenate([sin_signed] * reps, axis=-1)
    return x * cos + _rotate_half(x) * sin_signed


def _log_gamma(h):
    return math.log(1.0 - 2.0 ** (-5.0 - h))


def _head_of_keys(shape, dim):
    return lax.broadcasted_iota(jnp.int32, shape, dim) >> 6


def _layernorm(x, g):
    xc = x - jnp.mean(x, axis=-1, keepdims=True)
    return xc * lax.rsqrt(jnp.mean(xc * xc, axis=-1, keepdims=True) + EPS) * g


def _block_diag_mask():
    r = lax.broadcasted_iota(jnp.int32, (W_KEYS, W_BRANCH), 0) >> 6
    c = lax.broadcasted_iota(jnp.int32, (W_KEYS, W_BRANCH), 1) >> 7
    return r == c


def _head_blocks(full):
    return jnp.concatenate([full[h * DK:(h + 1) * DK, h * D_HEAD:(h + 1) * D_HEAD]
                            for h in range(N_HEADS)], axis=0)


GROUP = 256


def _linattn_kernel(bq_ref, bk_ref, bv_ref, bg_ref, cq_ref, ck_ref, cv_ref, cos_ref, sin_ref,
                    wup_ref, bgla_ref, nb_ref, nc_ref,
                    bo_ref, co_ref, sg_ref, sr_ref,
                    stg_sc, str_sc, *, rows):
    step = pl.program_id(0)

    @pl.when(step == 0)
    def _():
        stg_sc[...] = jnp.zeros(stg_sc.shape, F32)
        str_sc[...] = jnp.zeros(str_sc.shape, F32)

    n_chunks = rows // CHUNK
    n_groups = rows // GROUP
    cpg = GROUP // CHUNK
    ti = lax.broadcasted_iota(jnp.int32, (GROUP, GROUP), 0)
    si = lax.broadcasted_iota(jnp.int32, (GROUP, GROUP), 1)
    causal = ((ti >> 6) == (si >> 6)) & (ti >= si)
    tril = jnp.where(causal, 1.0, 0.0).astype(BF16)
    dist = (ti - si).astype(F32)
    key_head = _head_of_keys((GROUP, W_KEYS), 1)

    tcol = (lax.broadcasted_iota(jnp.int32, (GROUP, W_KEYS), 0) & (CHUNK - 1)).astype(F32)
    lg_lane = jnp.zeros((GROUP, W_KEYS), F32)
    lg_row = jnp.zeros((W_KEYS, D_HEAD), F32)
    row_head = _head_of_keys((W_KEYS, D_HEAD), 0)
    for h in range(N_HEADS):
        lg_lane = jnp.where(key_head == h, _log_gamma(h), lg_lane)
        lg_row = jnp.where(row_head == h, _log_gamma(h), lg_row)
    g_fwd = jnp.exp((tcol + 1.0) * lg_lane)
    g_rev = jnp.exp((CHUNK - 1.0 - tcol) * lg_lane)
    g_end = jnp.exp(CHUNK * lg_row)

    glog = _gla_log_decay(bg_ref[...], wup_ref, bgla_ref)
    cq_all = _rotary(cq_ref[...], cos_ref[...], sin_ref[...])
    ck_all = _rotary(ck_ref[...], cos_ref[...], sin_ref[...]) * (DK ** -0.5)
    bq_all = bq_ref[...] * (DK ** -0.5)

    g_qd, g_oi, g_upd, g_end_rows = [], [], [], []
    r_qf, r_oi, r_upd = [], [], []
    for g in range(n_groups):
        rs = slice(g * GROUP, (g + 1) * GROUP)
        g_hi, g_lo = _split_bf16(glog[rs])
        gcum = _dot(tril, g_hi) + _dot(tril, g_lo)
        ends = [gcum[(c + 1) * CHUNK - 1:(c + 1) * CHUNK] for c in range(cpg)]
        g_end_rows += ends
        gend_b = jnp.concatenate([jnp.broadcast_to(e, (CHUNK, W_KEYS)) for e in ends], axis=0)
        qd = bq_all[rs] * jnp.exp(gcum)
        bk = bk_ref[rs, :]
        kinv = (bk * jnp.exp(-gcum)).astype(BF16)
        krem = (bk * jnp.exp(gend_b - gcum)).astype(BF16)
        v = bv_ref[rs, :].astype(BF16)
        parts = []
        for h in range(N_HEADS):
            qm = jnp.where(key_head == h, qd, 0.0).astype(BF16)
            sc = jnp.where(causal, _dot_nt(qm, kinv), 0.0)
            parts.append(_dot(sc.astype(BF16), v[:, h * D_HEAD:(h + 1) * D_HEAD]))
        oi = jnp.concatenate(parts, axis=1)
        qdb = qd.astype(BF16)
        for c in range(cpg):
            cs = slice(c * CHUNK, (c + 1) * CHUNK)
            g_qd.append(qdb[cs])
            g_oi.append(oi[cs])
            g_upd.append(_head_blocks(_dot_tn(krem[cs], v[cs])))
        q = cq_all[rs]
        k = ck_all[rs]
        v = cv_ref[rs, :].astype(BF16)
        kb = k.astype(BF16)
        parts = []
        for h in range(N_HEADS):
            qm = jnp.where(key_head == h, q, 0.0).astype(BF16)
            decay = jnp.where(causal, jnp.exp(dist * _log_gamma(h)), 0.0)
            sc = _dot_nt(qm, kb) * decay
            parts.append(_dot(sc.astype(BF16), v[:, h * D_HEAD:(h + 1) * D_HEAD]))
        oi = jnp.concatenate(parts, axis=1)
        krev = (k * g_rev).astype(BF16)
        qfb = (q * g_fwd).astype(BF16)
        for c in range(cpg):
            cs = slice(c * CHUNK, (c + 1) * CHUNK)
            r_qf.append(qfb[cs])
            r_oi.append(oi[cs])
            r_upd.append(_head_blocks(_dot_tn(krev[cs], v[cs])))

    e_cols = jnp.exp(jnp.transpose(jnp.concatenate(g_end_rows, axis=0)))

    bd = _block_diag_mask()

    def state_products(qs, st):
        stb = st.astype(BF16)
        st_bd = jnp.where(bd, jnp.concatenate([stb] * N_HEADS, axis=1), jnp.zeros((), BF16))
        return _dot(qs, st_bd)

    st = stg_sc[...]
    for c in range(n_chunks):
        rs = slice(c * CHUNK, (c + 1) * CHUNK)
        o = g_oi[c] + state_products(g_qd[c], st)
        st = e_cols[:, c:c + 1] * st + g_upd[c]
        for h in range(N_HEADS):
            hs = slice(h * D_HEAD, (h + 1) * D_HEAD)
            bo_ref[rs, hs] = _rms(o[:, hs], nb_ref[:, hs]).astype(bo_ref.dtype)
    stg_sc[...] = st
    st = str_sc[...]
    for c in range(n_chunks):
        rs = slice(c * CHUNK, (c + 1) * CHUNK)
        o = r_oi[c] + state_products(r_qf[c], st)
        st = g_end * st + r_upd[c]
        for h in range(N_HEADS):
            hs = slice(h * D_HEAD, (h + 1) * D_HEAD)
            co_ref[rs, hs] = _layernorm(o[:, hs], nc_ref[:, hs]).astype(co_ref.dtype)
    str_sc[...] = st

    @pl.when(step == pl.num_programs(0) - 1)
    def _():
        for h in range(N_HEADS):
            sg_ref[h] = stg_sc[h * DK:(h + 1) * DK, :]
            sr_ref[h] = str_sc[h * DK:(h + 1) * DK, :]


def _prompt_linear_attention(p, cos, sin, wup, bgla, norm_b, norm_c):
    seq = p.shape[0]
    rows = min(512, seq)

    def col(c, w):
        return pl.BlockSpec((rows, w), lambda i: (i, c // w))

    const = lambda i: (0, 0)
    state = jax.ShapeDtypeStruct((N_HEADS, DK, D_HEAD), F32)
    state_spec = pl.BlockSpec((N_HEADS, DK, D_HEAD), lambda i: (0, 0, 0))
    return pl.pallas_call(
        functools.partial(_linattn_kernel, rows=rows),
        grid=(seq // rows,),
        in_specs=[col(C_BQ, W_KEYS), col(C_BK, W_KEYS), col(C_BV, W_BRANCH), col(C_BG, LANES),
                  col(C_CQ, W_KEYS), col(C_CK, W_KEYS), col(C_CV, W_BRANCH),
                  pl.BlockSpec((rows, LANES), lambda i: (i, 0)),
                  pl.BlockSpec((rows, LANES), lambda i: (i, 0)),
                  pl.BlockSpec((LANES, W_KEYS), const),
                  pl.BlockSpec((1, W_KEYS), const),
                  pl.BlockSpec((1, W_BRANCH), const),
                  pl.BlockSpec((1, W_BRANCH), const)],
        out_specs=[pl.BlockSpec((rows, W_BRANCH), lambda i: (i, 0)),
                   pl.BlockSpec((rows, W_BRANCH), lambda i: (i, 0)),
                   state_spec, state_spec],
        out_shape=[jax.ShapeDtypeStruct((seq, W_BRANCH), BF16),
                   jax.ShapeDtypeStruct((seq, W_BRANCH), BF16),
                   state, state],
        scratch_shapes=[pltpu.VMEM((W_KEYS, D_HEAD), F32),
                        pltpu.VMEM((W_KEYS, D_HEAD), F32)],
        compiler_params=_params(),
        name="prompt_linear_attention",
    )(p, p, p, p, p, p, p, cos, sin, wup, bgla, norm_b, norm_c)


def _to_column(x_row):
    n = x_row.shape[-1]
    r = lax.broadcasted_iota(jnp.int32, (n, n), 0)
    c = lax.broadcasted_iota(jnp.int32, (n, n), 1)
    return jnp.sum(jnp.where(r == c, jnp.broadcast_to(x_row, (n, n)), 0.0), axis=-1, keepdims=True)


def _recurrent_kernel(bq_ref, bk_ref, bv_ref, bg_ref, cq_ref, ck_ref, cv_ref, cos_ref, sin_ref,
                      wup_ref, bgla_ref, nb_ref, nc_ref, sg0_ref, sr0_ref,
                      bo_ref, co_ref, sg_ref, sr_ref):
    decay = _to_column(jnp.exp(_gla_log_decay(bg_ref[...], wup_ref, bgla_ref)))
    bq = _to_column(bq_ref[...] * (DK ** -0.5))
    bk = _to_column(bk_ref[...])
    cq = _to_column(_rotary(cq_ref[...], cos_ref[...], sin_ref[...]))
    ck = _to_column(_rotary(ck_ref[...], cos_ref[...], sin_ref[...]) * (DK ** -0.5))
    for h in range(N_HEADS):
        ks = slice(h * DK, (h + 1) * DK)
        hs = slice(h * D_HEAD, (h + 1) * D_HEAD)
        s_new = decay[ks] * sg0_ref[h] + bk[ks] * bv_ref[:, hs]
        sg_ref[h] = s_new
        o = jnp.sum(bq[ks] * s_new, axis=0, keepdims=True)
        bo_ref[:, hs] = _rms(o, nb_ref[:, hs]).astype(bo_ref.dtype)
        s_new = math.exp(_log_gamma(h)) * sr0_ref[h] + ck[ks] * cv_ref[:, hs]
        sr_ref[h] = s_new
        o = jnp.sum(cq[ks] * s_new, axis=0, keepdims=True)
        co_ref[:, hs] = _layernorm(o, nc_ref[:, hs]).astype(co_ref.dtype)


def _sample_linear_attention(p_s, cos, sin, wup, bgla, norm_b, norm_c, state_gla, state_ret, layer):
    db = p_s.shape[0]
    p3 = p_s.reshape(db, 1, N_MAIN)

    def col(c, w):
        return pl.BlockSpec((None, 1, w), lambda b: (b, 0, c // w))

    const = lambda b: (0, 0)
    state_spec = pl.BlockSpec((None, N_HEADS, DK, D_HEAD), lambda b: (b, 0, 0, 0))
    state_in_spec = pl.BlockSpec((None, None, N_HEADS, DK, D_HEAD), lambda b: (layer, b, 0, 0, 0))
    out_row = pl.BlockSpec((None, 1, W_BRANCH), lambda b: (b, 0, 0))
    state = jax.ShapeDtypeStruct((db, N_HEADS, DK, D_HEAD), F32)
    bo, co, sg, sr = pl.pallas_call(
        _recurrent_kernel,
        grid=(db,),
        in_specs=[col(C_BQ, W_KEYS), col(C_BK, W_KEYS), col(C_BV, W_BRANCH), col(C_BG, LANES),
                  col(C_CQ, W_KEYS), col(C_CK, W_KEYS), col(C_CV, W_BRANCH),
                  pl.BlockSpec((1, LANES), const), pl.BlockSpec((1, LANES), const),
                  pl.BlockSpec((LANES, W_KEYS), const),
                  pl.BlockSpec((1, W_KEYS), const),
                  pl.BlockSpec((1, W_BRANCH), const),
                  pl.BlockSpec((1, W_BRANCH), const),
                  state_in_spec, state_in_spec],
        out_specs=[out_row, out_row, state_spec, state_spec],
        out_shape=[jax.ShapeDtypeStruct((db, 1, W_BRANCH), BF16),
                   jax.ShapeDtypeStruct((db, 1, W_BRANCH), BF16),
                   state, state],
        compiler_params=_params(),
        name="sample_linear_attention",
    )(p3, p3, p3, p3, p3, p3, p3, cos, sin, wup, bgla, norm_b, norm_c, state_gla, state_ret)
    return bo.reshape(db, W_BRANCH), co.reshape(db, W_BRANCH), sg, sr


def _merge_kernel(x_ref, aot_ref, bo_ref, co_ref, br_ref, cg_ref, gates_ref,
                  wa_ref, wb_ref, wc_ref, wo_ref, o_ref):
    ya = _dot_tn(aot_ref[...], wa_ref[...])
    br = br_ref[...].astype(F32)
    yb = _dot((bo_ref[...].astype(F32) * (br * _sigmoid(br))).astype(BF16), wb_ref[...])
    cg = cg_ref[...].astype(F32)
    yc = _dot((co_ref[...].astype(F32) * (cg * _sigmoid(cg))).astype(BF16), wc_ref[...])
    mix = (_sigmoid(gates_ref[:, 0:D_MODEL].astype(F32)) * ya
           + _sigmoid(gates_ref[:, D_MODEL:2 * D_MODEL].astype(F32)) * yb
           + _sigmoid(gates_ref[:, 2 * D_MODEL:3 * D_MODEL].astype(F32)) * yc)
    o_ref[...] = x_ref[...] + _dot(mix.astype(BF16), wo_ref[...])


def _merge(x, aot, bo, co, p, wa, wb, wc, wo, layer):
    m = x.shape[0]
    tm = min(512, m)
    row = lambda i: (i, 0)
    of_layer = lambda i: (layer, 0, 0)
    return pl.pallas_call(
        _merge_kernel,
        grid=(m // tm,),
        in_specs=[pl.BlockSpec((tm, D_MODEL), row),
                  pl.BlockSpec((W_BRANCH, tm), lambda i: (0, i)),
                  pl.BlockSpec((tm, W_BRANCH), row),
                  pl.BlockSpec((tm, W_BRANCH), row),
                  pl.BlockSpec((tm, W_BRANCH), lambda i: (i, C_BR // W_BRANCH)),
                  pl.BlockSpec((tm, W_BRANCH), lambda i: (i, C_CG // W_BRANCH)),
                  pl.BlockSpec((tm, 3 * D_MODEL), lambda i: (i, C_GATES // (3 * D_MODEL))),
                  _resident((None, W_BRANCH, D_MODEL), of_layer),
                  _resident((None, W_BRANCH, D_MODEL), of_layer),
                  _resident((None, W_BRANCH, D_MODEL), of_layer),
                  _resident((None, D_MODEL, D_MODEL), of_layer)],
        out_specs=pl.BlockSpec((tm, D_MODEL), row),
        out_shape=jax.ShapeDtypeStruct((m, D_MODEL), F32),
        compiler_params=_params(),
        name="merge",
    )(x, aot, bo, co, p, p, p, wa, wb, wc, wo)


def _rope_tables(pos):
    half = DK // 2
    inv = ROPE_BASE ** (-jnp.arange(half, dtype=F32) / half)
    ang = pos.astype(F32)[:, None] * inv[None, :]
    cos = jnp.cos(ang)
    sin = jnp.sin(ang)
    cos_t = jnp.tile(jnp.concatenate([cos, cos], axis=-1), (1, LANES // DK))
    sin_t = jnp.tile(jnp.concatenate([-sin, sin], axis=-1), (1, LANES // DK))
    return cos_t, sin_t


def _reorder_w_in(w):
    s_a = 3 * W_BRANCH
    s_b = s_a + 2 * W_KEYS + 2 * W_BRANCH
    s_g = s_b + GLA_RANK
    s_c = s_g + 2 * W_KEYS + 2 * W_BRANCH
    s_br = s_b - W_BRANCH
    s_cg = s_c - W_BRANCH
    pad = jnp.zeros(w.shape[:-1] + (N_MAIN - C_BG - GLA_RANK,), F32)
    parts = [w[..., s_c:], w[..., s_br:s_b], w[..., s_cg:s_c],
             w[..., :s_br], w[..., s_g:s_cg], w[..., s_b:s_g], pad]
    return jnp.concatenate([part.astype(BF16) for part in parts], axis=-1)


def kernel(x_prompt, x_sample, cache_k, cache_v, state_gla, state_ret, page_table, w_in, w_gla_up, b_gla, lambda_qk, rel_bias, norm_a, norm_b, norm_c, w_a, w_b, w_c, w_out, norm_ffn1, ffn1_wg, ffn1_wu, ffn1_wd, norm_mix, norm_ffn2, ffn2_wg, ffn2_wu, ffn2_wd, norm_final):
    batch, seq, _ = x_prompt.shape
    db, dec_seq, _ = x_sample.shape
    depth = w_in.shape[0]
    assert batch == 1 and dec_seq == 1
    n_pages = page_table.shape[1]
    past = n_pages * PAGE_SIZE
    n_pool = cache_k.shape[1]
    n_pages_step = math.gcd(DECODE_PAGES_PER_STEP, n_pages)

    band, dec_bias = _bias_tables(rel_bias, past, n_pages_step * PAGE_SIZE)
    cos_p, sin_p = _rope_tables(jnp.arange(seq))
    cos_s, sin_s = _rope_tables(past + jnp.arange(dec_seq))
    ck4 = cache_k.reshape(depth, n_pool, PAGE_ROWS, D_HEAD)
    cv4 = cache_v.reshape(depth, n_pool, PAGE_ROWS, D_HEAD)
    g_final = norm_final.reshape(1, D_MODEL)

    xp = x_prompt.reshape(seq, D_MODEL)
    xs = x_sample.reshape(db, D_MODEL)
    w_in_b = _reorder_w_in(w_in)
    ffn1_w = tuple(w.astype(BF16) for w in (ffn1_wg, ffn1_wu, ffn1_wd))
    ffn2_w = tuple(w.astype(BF16) for w in (ffn2_wg, ffn2_wu, ffn2_wd))
    proj_w = tuple(w.astype(BF16) for w in (w_a, w_b, w_c, w_out))
    outs = {k: [] for k in ("kp", "vp", "gp", "rp", "ks", "vs", "gs", "rs")}
    for l in range(depth):
        lam_init = 0.8 - 0.6 * math.exp(-0.3 * l)
        last = l == depth - 1
        wup = jnp.concatenate([w_gla_up[l], jnp.zeros((LANES - GLA_RANK, W_KEYS), F32)], axis=0)
        bgla = b_gla[l].reshape(1, W_KEYS)
        nb = norm_b[l].reshape(1, W_BRANCH)
        nc = norm_c[l].reshape(1, W_BRANCH)
        na = norm_a[l].reshape(1, D_HEAD)
        na_col = norm_a[l].reshape(D_HEAD, 1)
        ffn1 = (norm_ffn1[l].reshape(1, D_MODEL), *ffn1_w, l)
        ffn2 = (norm_ffn2[l].reshape(1, D_MODEL), *ffn2_w, l)
        g_mix = norm_mix[l].reshape(1, D_MODEL)

        xp = _ffn_half(xp, *ffn1, g_final, False)
        p, gate_cols, qkv, k_new, v_new = _inproj(xp, g_mix, w_in_b, l)
        aot = _prompt_attention(qkv, band, lambda_qk[l], na_col, lam_init)
        bo, co, sg, sr = _prompt_linear_attention(p, cos_p, sin_p, wup, bgla, nb, nc)
        xp = _merge(xp, aot, bo, co, gate_cols, *proj_w, l)
        xp = _ffn_half(xp, *ffn2, g_final, last)
        outs["kp"].append(k_new[None])
        outs["vp"].append(v_new[None])
        outs["gp"].append(sg[None])
        outs["rp"].append(sr[None])

        xs = _ffn_half(xs, *ffn1, g_final, False)
        p_s, gate_cols, _, k_new, v_new = _inproj(xs, g_mix, w_in_b, l)
        ao = _sample_attention(p_s, ck4, cv4, l, page_table, dec_bias, rel_bias, lambda_qk[l], na,
                               lam_init, n_pages_step)
        bo, co, sg, sr = _sample_linear_attention(p_s, cos_s, sin_s, wup, bgla, nb, nc,
                                                  state_gla, state_ret, l)
        xs = _merge(xs, ao.T, bo, co, gate_cols, *proj_w, l)
        xs = _ffn_half(xs, *ffn2, g_final, last)
        outs["ks"].append(k_new[:, None])
        outs["vs"].append(v_new[:, None])
        outs["gs"].append(sg)
        outs["rs"].append(sr)

    st = lambda key: jnp.stack(outs[key])
    return (xp.reshape(batch, seq, D_MODEL), xs.reshape(db, dec_seq, D_MODEL),
            st("kp"), st("vp"), st("gp"), st("rp"), st("ks"), st("vs"), st("gs"), st("rs"))
```

```python
import functools
import math

import jax
import jax.numpy as jnp
from jax import lax
from jax.experimental import pallas as pl
from jax.experimental.pallas import tpu as pltpu

F32 = jnp.float32
BF16 = jnp.bfloat16

D_MODEL = 1024
N_HEADS = 4
DH_A = 64
D_HEAD = 128
DK = 64
GLA_RANK = 16
GLA_TAU = 16.0
ROPE_BASE = 10000.0
NUM_BUCKETS = 32
MAX_DISTANCE = 128
D_FF = 2816
PAGE_SIZE = 128
CHUNK = 64
EPS = 1e-6
NEG = -1e30
LOG2E = math.log2(math.e)
W_BRANCH = N_HEADS * D_HEAD
W_KEYS = N_HEADS * DK
PAGE_ROWS = PAGE_SIZE * N_HEADS

LANES = 128
VMEM_LIMIT_BYTES = 56 * 1024 * 1024

C_GATES = 0
C_BR = 3072
C_CG = 3584
N_GATE = 4096
C_AQ = 0
C_AK = 512
C_AV = 1024
C_BQ = 1536
C_BK = 1792
C_BV = 2048
C_CQ = 2560
C_CK = 2816
C_CV = 3072
C_BG = 3584
N_MAIN = 3712
N_PROJ = N_GATE + N_MAIN

FAR_DISTANCE = 113


def _params(n_axes=1, vmem=VMEM_LIMIT_BYTES):
    return pltpu.CompilerParams(dimension_semantics=("arbitrary",) * n_axes,
                                vmem_limit_bytes=vmem)


def _resident(shape, index_map):
    return pl.BlockSpec(shape, index_map, pipeline_mode=pl.Buffered(1))


def _rms(x, g):
    return x * lax.rsqrt(jnp.mean(x * x, axis=-1, keepdims=True) + EPS) * g


def _sigmoid(x):
    return 1.0 / (1.0 + jnp.exp(-x))


def _dot(a, b):
    return jnp.dot(a, b, preferred_element_type=F32)


def _dot_nt(a, b):
    return lax.dot_general(a, b, (((1,), (1,)), ((), ())), preferred_element_type=F32)


def _dot_tn(a, b):
    return lax.dot_general(a, b, (((0,), (0,)), ((), ())), preferred_element_type=F32)


FF_CHUNK = 1408


def _ffn_kernel(x_ref, g_ref, wg_ref, wu_ref, wd_ref, gf_ref, o_ref, *, final_norm):
    x = x_ref[...]
    n = _rms(x, g_ref[...]).astype(BF16)
    acc = None
    for c in range(0, D_FF, FF_CHUNK):
        a = _dot(n, wg_ref[:, c:c + FF_CHUNK])
        u = _dot(n, wu_ref[:, c:c + FF_CHUNK])
        h = (a * _sigmoid(a) * u).astype(BF16)
        part = _dot(h, wd_ref[c:c + FF_CHUNK, :])
        acc = part if acc is None else acc + part
    y = x + 0.5 * acc
    if final_norm:
        y = _rms(y, gf_ref[...])
    o_ref[...] = y


def _ffn_half(x, g, wg, wu, wd, layer, g_final, final_norm):
    m = x.shape[0]
    tm = min(512, m)
    row = lambda i: (i, 0)
    const = lambda i: (0, 0)
    of_layer = lambda i: (layer, 0, 0)
    return pl.pallas_call(
        functools.partial(_ffn_kernel, final_norm=final_norm),
        grid=(m // tm,),
        in_specs=[pl.BlockSpec((tm, D_MODEL), row),
                  _resident((1, D_MODEL), const),
                  _resident((None, D_MODEL, D_FF), of_layer),
                  _resident((None, D_MODEL, D_FF), of_layer),
                  _resident((None, D_FF, D_MODEL), of_layer),
                  _resident((1, D_MODEL), const)],
        out_specs=pl.BlockSpec((tm, D_MODEL), row),
        out_shape=jax.ShapeDtypeStruct((m, D_MODEL), F32),
        compiler_params=_params(),
        name="ffn_half",
    )(x, g, wg, wu, wd, g_final)


PROJ_CHUNK = 1024


def _inproj_kernel(x_ref, g_ref, w_ref, p_ref, gate_ref, qkv_ref, k_ref, v_ref):
    n = _rms(x_ref[...], g_ref[...]).astype(BF16)
    for c in range(0, N_GATE, PROJ_CHUNK):
        gate_ref[:, c:c + PROJ_CHUNK] = _dot(n, w_ref[:, c:c + PROJ_CHUNK]).astype(BF16)
    for c in range(0, N_MAIN, PROJ_CHUNK):
        w = min(PROJ_CHUNK, N_MAIN - c)
        p_ref[:, c:c + w] = _dot(n, w_ref[:, N_GATE + c:N_GATE + c + w])
    qkv_ref[:, 0:W_BRANCH] = (p_ref[:, C_AQ:C_AQ + W_BRANCH] * (DH_A ** -0.5 * LOG2E)).astype(BF16)
    qkv_ref[:, W_BRANCH:3 * W_BRANCH] = p_ref[:, C_AK:C_AK + 2 * W_BRANCH].astype(BF16)
    for h in range(N_HEADS):
        k_ref[:, h, :] = p_ref[:, C_AK + h * D_HEAD:C_AK + (h + 1) * D_HEAD]
        v_ref[:, h, :] = p_ref[:, C_AV + h * D_HEAD:C_AV + (h + 1) * D_HEAD]


def _inproj(x, g, w, layer):
    m = x.shape[0]
    tm = min(256, m)
    row = lambda i: (i, 0)
    const = lambda i: (0, 0)
    rows3 = pl.BlockSpec((tm, N_HEADS, D_HEAD), lambda i: (i, 0, 0))
    return pl.pallas_call(
        _inproj_kernel,
        grid=(m // tm,),
        in_specs=[pl.BlockSpec((tm, D_MODEL), row),
                  _resident((1, D_MODEL), const),
                  _resident((None, D_MODEL, N_PROJ), lambda i: (layer, 0, 0))],
        out_specs=[pl.BlockSpec((tm, N_MAIN), row),
                   pl.BlockSpec((tm, N_GATE), row),
                   pl.BlockSpec((tm, 3 * W_BRANCH), row),
                   rows3, rows3],
        out_shape=[jax.ShapeDtypeStruct((m, N_MAIN), F32),
                   jax.ShapeDtypeStruct((m, N_GATE), BF16),
                   jax.ShapeDtypeStruct((m, 3 * W_BRANCH), BF16),
                   jax.ShapeDtypeStruct((m, N_HEADS, D_HEAD), F32),
                   jax.ShapeDtypeStruct((m, N_HEADS, D_HEAD), F32)],
        compiler_params=_params(),
        name="inproj",
    )(x, g, w)


def _bucket_of(rel):
    n = jnp.maximum(rel, 0)
    max_exact = NUM_BUCKETS // 2
    nf = jnp.maximum(n, max_exact).astype(F32)
    large = max_exact + (jnp.log(nf / max_exact) / math.log(MAX_DISTANCE / max_exact)
                         * (NUM_BUCKETS - max_exact)).astype(jnp.int32)
    large = jnp.minimum(large, NUM_BUCKETS - 1)
    return jnp.where(n < max_exact, n, large)


def _bias_of(bucket, rb_ref, head):
    val = jnp.zeros(bucket.shape, F32)
    for b in range(NUM_BUCKETS):
        val = jnp.where(bucket == b, rb_ref[b, head], val)
    return val


def _bias_kernel(rb_ref, band_ref, dec_ref, *, past, group_width):
    kk = lax.broadcasted_iota(jnp.int32, (LANES, LANES), 0)
    qq = lax.broadcasted_iota(jnp.int32, (LANES, LANES), 1)
    for d in range(2):
        rel = qq - kk + d * LANES
        bucket = _bucket_of(rel)
        for h in range(N_HEADS):
            val = (_bias_of(bucket, rb_ref, h) - rb_ref[NUM_BUCKETS - 1, h]) * LOG2E
            band_ref[d, h] = jnp.where(rel >= 0, val, NEG)
    rows, width = dec_ref.shape
    row = lax.broadcasted_iota(jnp.int32, (rows, width), 0)
    col = lax.broadcasted_iota(jnp.int32, (rows, width), 1)
    rel = past - ((row >> 3) * group_width + (col >> 2))
    bucket = _bucket_of(rel)
    head = col & (N_HEADS - 1)
    own = head == (row & (N_HEADS - 1))
    val = jnp.zeros((rows, width), F32)
    for h in range(N_HEADS):
        val = jnp.where(head == h, _bias_of(bucket, rb_ref, h), val)
    dec_ref[...] = jnp.where(own, val, NEG)


def _bias_tables(rel_bias, past, group_width):
    n_groups = past // group_width
    return pl.pallas_call(
        functools.partial(_bias_kernel, past=past, group_width=group_width),
        in_specs=[pl.BlockSpec(memory_space=pltpu.SMEM)],
        out_shape=[jax.ShapeDtypeStruct((2, N_HEADS, LANES, LANES), F32),
                   jax.ShapeDtypeStruct((n_groups * 8, group_width * N_HEADS), F32)],
        compiler_params=pltpu.CompilerParams(vmem_limit_bytes=VMEM_LIMIT_BYTES),
        name="bias_tables",
    )(rel_bias)


def _lambda(lq_ref, lam_init):
    lq = lq_ref[...]
    a = jnp.sum(lq[0:1] * lq[1:2], axis=-1, keepdims=True)
    b = jnp.sum(lq[2:3] * lq[3:4], axis=-1, keepdims=True)
    return jnp.exp(a) - jnp.exp(b) + lam_init


def _first_map_lanes(rows):
    lane = lax.broadcasted_iota(jnp.int32, (rows, D_HEAD), 1)
    return lane < DH_A


SOFTMAX_STRIP = 2 * LANES


def _attn_kernel(q_ref, k_ref, v_ref, band_ref, lq_ref, na_ref, o_ref,
                 qbd_sc, m_sc, l_sc, acc_sc, s_sc, p_sc, a_sc, sn_sc, pn_sc, *, t, lam_init):
    i = pl.program_id(0)
    nsub = t // LANES
    first_map = _first_map_lanes(t)

    for h in range(N_HEADS):
        q_h = q_ref[:, h * D_HEAD:(h + 1) * D_HEAD]
        zero = jnp.zeros_like(q_h)
        qbd_sc[h, 0:t, :] = jnp.where(first_map, q_h, zero)
        qbd_sc[h, t:2 * t, :] = jnp.where(first_map, zero, q_h)
    m_sc[...] = jnp.full(m_sc.shape, NEG, F32)
    l_sc[...] = jnp.zeros(l_sc.shape, F32)
    acc_sc[...] = jnp.zeros(acc_sc.shape, F32)

    def stage_scores(h, ks, slot, lead_of, s_buf):
        s = _dot_nt(k_ref[pl.ds(ks, t), h * D_HEAD:(h + 1) * D_HEAD], qbd_sc[h])
        if lead_of is not None:
            rows = []
            for kc in range(nsub):
                cols = []
                for mp in range(2):
                    for qr in range(nsub):
                        sub = s[kc * LANES:(kc + 1) * LANES, mp * t + qr * LANES:mp * t + (qr + 1) * LANES]
                        lead = lead_of(kc, qr)
                        if lead < 0:
                            sub = jnp.full_like(sub, NEG)
                        elif lead < 2:
                            sub = sub + band_ref[lead, h]
                        cols.append(sub)
                rows.append(jnp.concatenate(cols, axis=1))
            s = jnp.concatenate(rows, axis=0) if nsub > 1 else rows[0]
        s_buf[slot] = s

    def stage_softmax(h, slot, s_buf, p_buf):
        m_prev = m_sc[h]
        m_new = jnp.maximum(m_prev, jnp.max(s_buf[slot], axis=0, keepdims=True))
        alpha = jnp.exp2(m_prev - m_new)
        sums = []
        for c0 in range(0, 2 * t, SOFTMAX_STRIP):
            cs = slice(c0, c0 + SOFTMAX_STRIP)
            p = jnp.exp2(s_buf[slot, :, cs] - m_new[:, cs])
            sums.append(jnp.sum(p, axis=0, keepdims=True))
            p_buf[slot, :, cs] = p.astype(BF16)
        l_sc[h] = alpha * l_sc[h] + jnp.concatenate(sums, axis=1)
        a_sc[h] = alpha
        m_sc[h] = m_new

    def stage_values(h, ks, slot, p_buf):
        v_h = v_ref[pl.ds(ks, t), h * D_HEAD:(h + 1) * D_HEAD]
        acc_sc[h] = a_sc[h] * acc_sc[h] + _dot_tn(v_h, p_buf[slot])

    def kv_block(ks, lead_of, s_buf, p_buf):
        stage_scores(0, ks, 0, lead_of, s_buf)
        for h in range(N_HEADS):
            if h + 1 < N_HEADS:
                stage_scores(h + 1, ks, (h + 1) % 2, lead_of, s_buf)
            stage_softmax(h, h % 2, s_buf, p_buf)
            stage_values(h, ks, h % 2, p_buf)

    def far_pair(j, carry):
        ks0 = pl.multiple_of(j * (2 * t), 2 * t)
        steps = [(h, pl.multiple_of(ks0 + blk * t, t)) for blk in range(2) for h in range(N_HEADS)]
        stage_scores(steps[0][0], steps[0][1], 0, None, s_sc)
        for n, (h, ks) in enumerate(steps):
            if n + 1 < len(steps):
                stage_scores(steps[n + 1][0], steps[n + 1][1], (n + 1) % 2, None, s_sc)
            stage_softmax(h, n % 2, s_sc, p_sc)
            stage_values(h, ks, n % 2, p_sc)
        return carry

    n_far = jnp.maximum(i - 1, 0)
    lax.fori_loop(0, n_far // 2, far_pair, 0)

    @pl.when(n_far % 2 == 1)
    def _():
        kv_block(pl.multiple_of((n_far - 1) * t, t), None, sn_sc, pn_sc)

    @pl.when(i >= 1)
    def _():
        kv_block(pl.multiple_of((i - 1) * t, t), lambda kc, qr: qr - kc + nsub, sn_sc, pn_sc)

    kv_block(pl.multiple_of(i * t, t), lambda kc, qr: qr - kc, sn_sc, pn_sc)
    lam = _lambda(lq_ref, lam_init)
    for h in range(N_HEADS):
        o = acc_sc[h] / l_sc[h]
        attn = o[:, 0:t] - lam * o[:, t:2 * t]
        norm = lax.rsqrt(jnp.mean(attn * attn, axis=0, keepdims=True) + EPS)
        o_ref[h * D_HEAD:(h + 1) * D_HEAD, :] = (
            attn * norm * na_ref[...] * (1.0 - lam_init)).astype(o_ref.dtype)


def _prompt_attention(qkv, band, lambda_qk, norm_a_col, lam_init):
    seq = qkv.shape[0]
    t = min(512, seq)
    assert t % LANES == 0 and seq % t == 0 and t + 1 >= FAR_DISTANCE
    const2 = lambda i: (0, 0)
    return pl.pallas_call(
        functools.partial(_attn_kernel, t=t, lam_init=lam_init),
        grid=(seq // t,),
        in_specs=[pl.BlockSpec((t, W_BRANCH), lambda i: (i, 0)),
                  _resident((seq, W_BRANCH), lambda i: (0, 1)),
                  _resident((seq, W_BRANCH), lambda i: (0, 2)),
                  _resident((2, N_HEADS, LANES, LANES), lambda i: (0, 0, 0, 0)),
                  _resident((4, DH_A), const2),
                  _resident((D_HEAD, 1), const2)],
        out_specs=pl.BlockSpec((W_BRANCH, t), lambda i: (0, i)),
        out_shape=jax.ShapeDtypeStruct((W_BRANCH, seq), BF16),
        scratch_shapes=[pltpu.VMEM((N_HEADS, 2 * t, D_HEAD), BF16),
                        pltpu.VMEM((N_HEADS, 1, 2 * t), F32),
                        pltpu.VMEM((N_HEADS, 1, 2 * t), F32),
                        pltpu.VMEM((N_HEADS, D_HEAD, 2 * t), F32),
                        pltpu.VMEM((2, t, 2 * t), F32),
                        pltpu.VMEM((2, t, 2 * t), BF16),
                        pltpu.VMEM((N_HEADS, 1, 2 * t), F32),
                        pltpu.VMEM((2, t, 2 * t), F32),
                        pltpu.VMEM((2, t, 2 * t), BF16)],
        compiler_params=_params(),
        name="prompt_attention",
    )(qkv, qkv, qkv, band, lambda_qk, norm_a_col)


DECODE_PAGES_PER_STEP = 32


def _rows_by_head(x_row):
    pieces = [x_row[:, h * D_HEAD:(h + 1) * D_HEAD] for h in range(N_HEADS)]
    return jnp.concatenate(pieces + pieces, axis=0)


def _decode_kernel(pt_ref, ck_hbm, cv_hbm, dec_ref, rb_ref, lq_ref, na_ref, q_ref, ks_ref, vs_ref, o_ref,
                   kbuf, vbuf, sems, qrows_sc, m_sc, l_sc, acc_sc, *, layer, n_pages_step, lam_init):
    b = pl.program_id(0)
    g = pl.program_id(1)
    n_b = pl.num_programs(0)
    n_groups = pl.num_programs(1)
    step = b * n_groups + g
    slot = lax.rem(step, 2)

    def page_copies(bb, gg, sl):
        copies = []
        for r in range(n_pages_step):
            page = pt_ref[bb, gg * n_pages_step + r]
            copies.append(pltpu.make_async_copy(ck_hbm.at[layer, page], kbuf.at[sl, r], sems.at[sl, 0]))
            copies.append(pltpu.make_async_copy(cv_hbm.at[layer, page], vbuf.at[sl, r], sems.at[sl, 1]))
        return copies

    @pl.when(step == 0)
    def _():
        for cp in page_copies(b, g, slot):
            cp.start()

    wraps = g == n_groups - 1
    next_b = jnp.where(wraps, b + 1, b)
    next_g = jnp.where(wraps, 0, g + 1)

    @pl.when(step + 1 < n_b * n_groups)
    def _():
        for cp in page_copies(next_b, next_g, 1 - slot):
            cp.start()

    n_rows = 2 * N_HEADS
    row = lax.broadcasted_iota(jnp.int32, (n_rows, D_HEAD), 0)
    lane = lax.broadcasted_iota(jnp.int32, (n_rows, D_HEAD), 1)
    own_map = (lane >> 6) == (row >> 2)

    @pl.when(g == 0)
    def _():
        q8 = _rows_by_head(q_ref[...] * (DH_A ** -0.5))
        qrows_sc[...] = jnp.where(own_map, q8, 0.0)
        m_sc[...] = jnp.full(m_sc.shape, NEG, F32)
        l_sc[...] = jnp.zeros(l_sc.shape, F32)
        acc_sc[...] = jnp.zeros(acc_sc.shape, F32)

    qrows = qrows_sc[...]
    qb = qrows.astype(BF16)

    for cp in page_copies(b, g, slot):
        cp.wait()
    k_pages = [kbuf[slot, r] for r in range(n_pages_step)]
    v_pages = [vbuf[slot, r] for r in range(n_pages_step)]

    s = jnp.concatenate([_dot_nt(qb, k_pages[r].astype(BF16)) for r in range(n_pages_step)], axis=1)
    s = s + dec_ref[...]
    m_prev = m_sc[...]
    m_new = jnp.maximum(m_prev, jnp.max(s, axis=-1, keepdims=True))
    alpha = jnp.exp(m_prev - m_new)
    p = jnp.exp(s - m_new)
    l_sc[...] = alpha * l_sc[...] + jnp.sum(p, axis=-1, keepdims=True)
    pb = p.astype(BF16)
    pv = None
    for r in range(n_pages_step):
        part = _dot(pb[:, r * PAGE_ROWS:(r + 1) * PAGE_ROWS], v_pages[r].astype(BF16))
        pv = part if pv is None else pv + part
    acc_sc[...] = alpha * acc_sc[...] + pv
    m_sc[...] = m_new

    @pl.when(g == pl.num_programs(1) - 1)
    def _():
        head = lax.broadcasted_iota(jnp.int32, (n_rows, 1), 0) & (N_HEADS - 1)
        b0 = jnp.zeros((n_rows, 1), F32)
        for h in range(N_HEADS):
            b0 = jnp.where(head == h, rb_ref[0, h], b0)
        s_own = jnp.sum(qrows * _rows_by_head(ks_ref[...]), axis=-1, keepdims=True) + b0
        m_prev = m_sc[...]
        m_fin = jnp.maximum(m_prev, s_own)
        alpha = jnp.exp(m_prev - m_fin)
        p_own = jnp.exp(s_own - m_fin)
        l_fin = alpha * l_sc[...] + p_own
        acc = alpha * acc_sc[...] + p_own * _rows_by_head(vs_ref[...])
        o = acc / l_fin
        lam = _lambda(lq_ref, lam_init)
        attn = o[0:N_HEADS] - lam * o[N_HEADS:n_rows]
        out = _rms(attn, na_ref[...]) * (1.0 - lam_init)
        for h in range(N_HEADS):
            o_ref[:, h * D_HEAD:(h + 1) * D_HEAD] = out[h:h + 1].astype(o_ref.dtype)


def _sample_attention(p_s, cache_k, cache_v, layer, page_table, dec_bias, rel_bias, lambda_qk, norm_a,
                      lam_init, n_pages_step):
    db = p_s.shape[0]
    n_pages = page_table.shape[1]
    n_groups = n_pages // n_pages_step
    gw = n_pages_step * PAGE_ROWS
    p3 = p_s.reshape(db, 1, N_MAIN)
    dec3 = dec_bias.reshape(n_groups, 8, gw)

    def row_spec(col):
        return pl.BlockSpec((None, 1, W_BRANCH), lambda b, g, pt: (b, 0, col // W_BRANCH))

    const2 = lambda b, g, pt: (0, 0)
    page_buffers = pltpu.VMEM((2, n_pages_step, PAGE_ROWS, D_HEAD), F32)
    grid_spec = pltpu.PrefetchScalarGridSpec(
        num_scalar_prefetch=1,
        grid=(db, n_groups),
        in_specs=[pl.BlockSpec(memory_space=pl.ANY),
                  pl.BlockSpec(memory_space=pl.ANY),
                  pl.BlockSpec((None, 8, gw), lambda b, g, pt: (g, 0, 0)),
                  pl.BlockSpec(memory_space=pltpu.SMEM),
                  pl.BlockSpec((4, DH_A), const2),
                  pl.BlockSpec((1, D_HEAD), const2),
                  row_spec(C_AQ), row_spec(C_AK), row_spec(C_AV)],
        out_specs=pl.BlockSpec((None, 1, W_BRANCH), lambda b, g, pt: (b, 0, 0)),
        scratch_shapes=[page_buffers, page_buffers,
                        pltpu.SemaphoreType.DMA((2, 2)),
                        pltpu.VMEM((8, D_HEAD), F32),
                        pltpu.VMEM((8, 1), F32),
                        pltpu.VMEM((8, 1), F32),
                        pltpu.VMEM((8, D_HEAD), F32)])
    out = pl.pallas_call(
        functools.partial(_decode_kernel, layer=layer, n_pages_step=n_pages_step, lam_init=lam_init),
        grid_spec=grid_spec,
        out_shape=jax.ShapeDtypeStruct((db, 1, W_BRANCH), BF16),
        compiler_params=_params(2),
        name="sample_attention",
    )(page_table, cache_k, cache_v, dec3, rel_bias, lambda_qk, norm_a, p3, p3, p3)
    return out.reshape(db, W_BRANCH)


def _log_sigmoid(x):
    return jnp.minimum(x, 0.0) - jnp.log1p(jnp.exp(-jnp.abs(x)))


def _split_bf16(x):
    hi = x.astype(BF16)
    lo = (x - hi.astype(F32)).astype(BF16)
    return hi, lo


def _dot_3pass(a, b):
    a_hi, a_lo = _split_bf16(a)
    b_hi, b_lo = _split_bf16(b)
    return _dot(a_hi, b_hi) + (_dot(a_hi, b_lo) + _dot(a_lo, b_hi))


def _gla_log_decay(bg, wup_ref, bgla_ref):
    x = _dot_3pass(bg, wup_ref[...])
    return _log_sigmoid(x + bgla_ref[...]) / GLA_TAU


def _rotate_half(x):
    width = x.shape[-1]
    lane = lax.broadcasted_iota(jnp.int32, x.shape, x.ndim - 1)
    lo = (lane & (DK - 1)) < DK // 2
    return jnp.where(lo, pltpu.roll(x, width - DK // 2, x.ndim - 1), pltpu.roll(x, DK // 2, x.ndim - 1))


def _rotary(x, cos, sin_signed):
    reps = x.shape[-1] // cos.shape[-1]
    cos = jnp.concatenate([cos] * reps, axis=-1)
    sin_signed = jnp.concatenate([sin_signed] * reps, axis=-1)
    return x * cos + _rotate_half(x) * sin_signed


def _log_gamma(h):
    return math.log(1.0 - 2.0 ** (-5.0 - h))


def _head_of_keys(shape, dim):
    return lax.broadcasted_iota(jnp.int32, shape, dim) >> 6


def _layernorm(x, g):
    xc = x - jnp.mean(x, axis=-1, keepdims=True)
    return xc * lax.rsqrt(jnp.mean(xc * xc, axis=-1, keepdims=True) + EPS) * g


def _block_diag_mask():
    r = lax.broadcasted_iota(jnp.int32, (W_KEYS, W_BRANCH), 0) >> 6
    c = lax.broadcasted_iota(jnp.int32, (W_KEYS, W_BRANCH), 1) >> 7
    return r == c


def _head_blocks(full):
    return jnp.concatenate([full[h * DK:(h + 1) * DK, h * D_HEAD:(h + 1) * D_HEAD]
                            for h in range(N_HEADS)], axis=0)


GROUP = 256


def _linattn_kernel(bq_ref, bk_ref, bv_ref, bg_ref, cq_ref, ck_ref, cv_ref, cos_ref, sin_ref,
                    wup_ref, bgla_ref, nb_ref, nc_ref,
                    bo_ref, co_ref, sg_ref, sr_ref,
                    stg_sc, str_sc, *, rows):
    step = pl.program_id(0)

    @pl.when(step == 0)
    def _():
        stg_sc[...] = jnp.zeros(stg_sc.shape, F32)
        str_sc[...] = jnp.zeros(str_sc.shape, F32)

    n_chunks = rows // CHUNK
    n_groups = rows // GROUP
    cpg = GROUP // CHUNK
    ti = lax.broadcasted_iota(jnp.int32, (GROUP, GROUP), 0)
    si = lax.broadcasted_iota(jnp.int32, (GROUP, GROUP), 1)
    causal = ((ti >> 6) == (si >> 6)) & (ti >= si)
    tril = jnp.where(causal, 1.0, 0.0).astype(BF16)
    dist = (ti - si).astype(F32)
    key_head = _head_of_keys((GROUP, W_KEYS), 1)

    tcol = (lax.broadcasted_iota(jnp.int32, (GROUP, W_KEYS), 0) & (CHUNK - 1)).astype(F32)
    lg_lane = jnp.zeros((GROUP, W_KEYS), F32)
    lg_row = jnp.zeros((W_KEYS, D_HEAD), F32)
    row_head = _head_of_keys((W_KEYS, D_HEAD), 0)
    for h in range(N_HEADS):
        lg_lane = jnp.where(key_head == h, _log_gamma(h), lg_lane)
        lg_row = jnp.where(row_head == h, _log_gamma(h), lg_row)
    g_fwd = jnp.exp((tcol + 1.0) * lg_lane)
    g_rev = jnp.exp((CHUNK - 1.0 - tcol) * lg_lane)
    g_end = jnp.exp(CHUNK * lg_row)

    glog = _gla_log_decay(bg_ref[...], wup_ref, bgla_ref)
    cq_all = _rotary(cq_ref[...], cos_ref[...], sin_ref[...])
    ck_all = _rotary(ck_ref[...], cos_ref[...], sin_ref[...]) * (DK ** -0.5)
    bq_all = bq_ref[...] * (DK ** -0.5)

    g_qd, g_oi, g_upd, g_end_rows = [], [], [], []
    r_qf, r_oi, r_upd = [], [], []
    for g in range(n_groups):
        rs = slice(g * GROUP, (g + 1) * GROUP)
        g_hi, g_lo = _split_bf16(glog[rs])
        gcum = _dot(tril, g_hi) + _dot(tril, g_lo)
        ends = [gcum[(c + 1) * CHUNK - 1:(c + 1) * CHUNK] for c in range(cpg)]
        g_end_rows += ends
        gend_b = jnp.concatenate([jnp.broadcast_to(e, (CHUNK, W_KEYS)) for e in ends], axis=0)
        qd = bq_all[rs] * jnp.exp(gcum)
        bk = bk_ref[rs, :]
        kinv = (bk * jnp.exp(-gcum)).astype(BF16)
        krem = (bk * jnp.exp(gend_b - gcum)).astype(BF16)
        v = bv_ref[rs, :].astype(BF16)
        parts = []
        for h in range(N_HEADS):
            qm = jnp.where(key_head == h, qd, 0.0).astype(BF16)
            sc = jnp.where(causal, _dot_nt(qm, kinv), 0.0)
            parts.append(_dot(sc.astype(BF16), v[:, h * D_HEAD:(h + 1) * D_HEAD]))
        oi = jnp.concatenate(parts, axis=1)
        qdb = qd.astype(BF16)
        for c in range(cpg):
            cs = slice(c * CHUNK, (c + 1) * CHUNK)
            g_qd.append(qdb[cs])
            g_oi.append(oi[cs])
            g_upd.append(_head_blocks(_dot_tn(krem[cs], v[cs])))
        q = cq_all[rs]
        k = ck_all[rs]
        v = cv_ref[rs, :].astype(BF16)
        kb = k.astype(BF16)
        parts = []
        for h in range(N_HEADS):
            qm = jnp.where(key_head == h, q, 0.0).astype(BF16)
            decay = jnp.where(causal, jnp.exp(dist * _log_gamma(h)), 0.0)
            sc = _dot_nt(qm, kb) * decay
            parts.append(_dot(sc.astype(BF16), v[:, h * D_HEAD:(h + 1) * D_HEAD]))
        oi = jnp.concatenate(parts, axis=1)
        krev = (k * g_rev).astype(BF16)
        qfb = (q * g_fwd).astype(BF16)
        for c in range(cpg):
            cs = slice(c * CHUNK, (c + 1) * CHUNK)
            r_qf.append(qfb[cs])
            r_oi.append(oi[cs])
            r_upd.append(_head_blocks(_dot_tn(krev[cs], v[cs])))

    e_cols = jnp.exp(jnp.transpose(jnp.concatenate(g_end_rows, axis=0)))

    bd = _block_diag_mask()

    def state_products(qs, st):
        stb = st.astype(BF16)
        st_bd = jnp.where(bd, jnp.concatenate([stb] * N_HEADS, axis=1), jnp.zeros((), BF16))
        return _dot(qs, st_bd)

    st = stg_sc[...]
    for c in range(n_chunks):
        rs = slice(c * CHUNK, (c + 1) * CHUNK)
        o = g_oi[c] + state_products(g_qd[c], st)
        st = e_cols[:, c:c + 1] * st + g_upd[c]
        for h in range(N_HEADS):
            hs = slice(h * D_HEAD, (h + 1) * D_HEAD)
            bo_ref[rs, hs] = _rms(o[:, hs], nb_ref[:, hs]).astype(bo_ref.dtype)
    stg_sc[...] = st
    st = str_sc[...]
    for c in range(n_chunks):
        rs = slice(c * CHUNK, (c + 1) * CHUNK)
        o = r_oi[c] + state_products(r_qf[c], st)
        st = g_end * st + r_upd[c]
        for h in range(N_HEADS):
            hs = slice(h * D_HEAD, (h + 1) * D_HEAD)
            co_ref[rs, hs] = _layernorm(o[:, hs], nc_ref[:, hs]).astype(co_ref.dtype)
    str_sc[...] = st

    @pl.when(step == pl.num_programs(0) - 1)
    def _():
        for h in range(N_HEADS):
            sg_ref[h] = stg_sc[h * DK:(h + 1) * DK, :]
            sr_ref[h] = str_sc[h * DK:(h + 1) * DK, :]


def _prompt_linear_attention(p, cos, sin, wup, bgla, norm_b, norm_c):
    seq = p.shape[0]
    rows = min(512, seq)

    def col(c, w):
        return pl.BlockSpec((rows, w), lambda i: (i, c // w))

    const = lambda i: (0, 0)
    state = jax.ShapeDtypeStruct((N_HEADS, DK, D_HEAD), F32)
    state_spec = pl.BlockSpec((N_HEADS, DK, D_HEAD), lambda i: (0, 0, 0))
    return pl.pallas_call(
        functools.partial(_linattn_kernel, rows=rows),
        grid=(seq // rows,),
        in_specs=[col(C_BQ, W_KEYS), col(C_BK, W_KEYS), col(C_BV, W_BRANCH), col(C_BG, LANES),
                  col(C_CQ, W_KEYS), col(C_CK, W_KEYS), col(C_CV, W_BRANCH),
                  pl.BlockSpec((rows, LANES), lambda i: (i, 0)),
                  pl.BlockSpec((rows, LANES), lambda i: (i, 0)),
                  pl.BlockSpec((LANES, W_KEYS), const),
                  pl.BlockSpec((1, W_KEYS), const),
                  pl.BlockSpec((1, W_BRANCH), const),
                  pl.BlockSpec((1, W_BRANCH), const)],
        out_specs=[pl.BlockSpec((rows, W_BRANCH), lambda i: (i, 0)),
                   pl.BlockSpec((rows, W_BRANCH), lambda i: (i, 0)),
                   state_spec, state_spec],
        out_shape=[jax.ShapeDtypeStruct((seq, W_BRANCH), BF16),
                   jax.ShapeDtypeStruct((seq, W_BRANCH), BF16),
                   state, state],
        scratch_shapes=[pltpu.VMEM((W_KEYS, D_HEAD), F32),
                        pltpu.VMEM((W_KEYS, D_HEAD), F32)],
        compiler_params=_params(),
        name="prompt_linear_attention",
    )(p, p, p, p, p, p, p, cos, sin, wup, bgla, norm_b, norm_c)


def _to_column(x_row):
    n = x_row.shape[-1]
    r = lax.broadcasted_iota(jnp.int32, (n, n), 0)
    c = lax.broadcasted_iota(jnp.int32, (n, n), 1)
    return jnp.sum(jnp.where(r == c, jnp.broadcast_to(x_row, (n, n)), 0.0), axis=-1, keepdims=True)


def _recurrent_kernel(bq_ref, bk_ref, bv_ref, bg_ref, cq_ref, ck_ref, cv_ref, cos_ref, sin_ref,
                      wup_ref, bgla_ref, nb_ref, nc_ref, sg0_ref, sr0_ref,
                      bo_ref, co_ref, sg_ref, sr_ref):
    decay = _to_column(jnp.exp(_gla_log_decay(bg_ref[...], wup_ref, bgla_ref)))
    bq = _to_column(bq_ref[...] * (DK ** -0.5))
    bk = _to_column(bk_ref[...])
    cq = _to_column(_rotary(cq_ref[...], cos_ref[...], sin_ref[...]))
    ck = _to_column(_rotary(ck_ref[...], cos_ref[...], sin_ref[...]) * (DK ** -0.5))
    for h in range(N_HEADS):
        ks = slice(h * DK, (h + 1) * DK)
        hs = slice(h * D_HEAD, (h + 1) * D_HEAD)
        s_new = decay[ks] * sg0_ref[h] + bk[ks] * bv_ref[:, hs]
        sg_ref[h] = s_new
        o = jnp.sum(bq[ks] * s_new, axis=0, keepdims=True)
        bo_ref[:, hs] = _rms(o, nb_ref[:, hs]).astype(bo_ref.dtype)
        s_new = math.exp(_log_gamma(h)) * sr0_ref[h] + ck[ks] * cv_ref[:, hs]
        sr_ref[h] = s_new
        o = jnp.sum(cq[ks] * s_new, axis=0, keepdims=True)
        co_ref[:, hs] = _layernorm(o, nc_ref[:, hs]).astype(co_ref.dtype)


def _sample_linear_attention(p_s, cos, sin, wup, bgla, norm_b, norm_c, state_gla, state_ret, layer):
    db = p_s.shape[0]
    p3 = p_s.reshape(db, 1, N_MAIN)

    def col(c, w):
        return pl.BlockSpec((None, 1, w), lambda b: (b, 0, c // w))

    const = lambda b: (0, 0)
    state_spec = pl.BlockSpec((None, N_HEADS, DK, D_HEAD), lambda b: (b, 0, 0, 0))
    state_in_spec = pl.BlockSpec((None, None, N_HEADS, DK, D_HEAD), lambda b: (layer, b, 0, 0, 0))
    out_row = pl.BlockSpec((None, 1, W_BRANCH), lambda b: (b, 0, 0))
    state = jax.ShapeDtypeStruct((db, N_HEADS, DK, D_HEAD), F32)
    bo, co, sg, sr = pl.pallas_call(
        _recurrent_kernel,
        grid=(db,),
        in_specs=[col(C_BQ, W_KEYS), col(C_BK, W_KEYS), col(C_BV, W_BRANCH), col(C_BG, LANES),
                  col(C_CQ, W_KEYS), col(C_CK, W_KEYS), col(C_CV, W_BRANCH),
                  pl.BlockSpec((1, LANES), const), pl.BlockSpec((1, LANES), const),
                  pl.BlockSpec((LANES, W_KEYS), const),
                  pl.BlockSpec((1, W_KEYS), const),
                  pl.BlockSpec((1, W_BRANCH), const),
                  pl.BlockSpec((1, W_BRANCH), const),
                  state_in_spec, state_in_spec],
        out_specs=[out_row, out_row, state_spec, state_spec],
        out_shape=[jax.ShapeDtypeStruct((db, 1, W_BRANCH), BF16),
                   jax.ShapeDtypeStruct((db, 1, W_BRANCH), BF16),
                   state, state],
        compiler_params=_params(),
        name="sample_linear_attention",
    )(p3, p3, p3, p3, p3, p3, p3, cos, sin, wup, bgla, norm_b, norm_c, state_gla, state_ret)
    return bo.reshape(db, W_BRANCH), co.reshape(db, W_BRANCH), sg, sr


def _merge_kernel(x_ref, aot_ref, bo_ref, co_ref, br_ref, cg_ref, gates_ref,
                  wa_ref, wb_ref, wc_ref, wo_ref, o_ref):
    ya = _dot_tn(aot_ref[...], wa_ref[...])
    br = br_ref[...].astype(F32)
    yb = _dot((bo_ref[...].astype(F32) * (br * _sigmoid(br))).astype(BF16), wb_ref[...])
    cg = cg_ref[...].astype(F32)
    yc = _dot((co_ref[...].astype(F32) * (cg * _sigmoid(cg))).astype(BF16), wc_ref[...])
    mix = (_sigmoid(gates_ref[:, 0:D_MODEL].astype(F32)) * ya
           + _sigmoid(gates_ref[:, D_MODEL:2 * D_MODEL].astype(F32)) * yb
           + _sigmoid(gates_ref[:, 2 * D_MODEL:3 * D_MODEL].astype(F32)) * yc)
    o_ref[...] = x_ref[...] + _dot(mix.astype(BF16), wo_ref[...])


def _merge(x, aot, bo, co, p, wa, wb, wc, wo, layer):
    m = x.shape[0]
    tm = min(512, m)
    row = lambda i: (i, 0)
    of_layer = lambda i: (layer, 0, 0)
    return pl.pallas_call(
        _merge_kernel,
        grid=(m // tm,),
        in_specs=[pl.BlockSpec((tm, D_MODEL), row),
                  pl.BlockSpec((W_BRANCH, tm), lambda i: (0, i)),
                  pl.BlockSpec((tm, W_BRANCH), row),
                  pl.BlockSpec((tm, W_BRANCH), row),
                  pl.BlockSpec((tm, W_BRANCH), lambda i: (i, C_BR // W_BRANCH)),
                  pl.BlockSpec((tm, W_BRANCH), lambda i: (i, C_CG // W_BRANCH)),
                  pl.BlockSpec((tm, 3 * D_MODEL), lambda i: (i, C_GATES // (3 * D_MODEL))),
                  _resident((None, W_BRANCH, D_MODEL), of_layer),
                  _resident((None, W_BRANCH, D_MODEL), of_layer),
                  _resident((None, W_BRANCH, D_MODEL), of_layer),
                  _resident((None, D_MODEL, D_MODEL), of_layer)],
        out_specs=pl.BlockSpec((tm, D_MODEL), row),
        out_shape=jax.ShapeDtypeStruct((m, D_MODEL), F32),
        compiler_params=_params(),
        name="merge",
    )(x, aot, bo, co, p, p, p, wa, wb, wc, wo)


def _rope_tables(pos):
    half = DK // 2
    inv = ROPE_BASE ** (-jnp.arange(half, dtype=F32) / half)
    ang = pos.astype(F32)[:, None] * inv[None, :]
    cos = jnp.cos(ang)
    sin = jnp.sin(ang)
    cos_t = jnp.tile(jnp.concatenate([cos, cos], axis=-1), (1, LANES // DK))
    sin_t = jnp.tile(jnp.concatenate([-sin, sin], axis=-1), (1, LANES // DK))
    return cos_t, sin_t


def _reorder_w_in(w):
    s_a = 3 * W_BRANCH
    s_b = s_a + 2 * W_KEYS + 2 * W_BRANCH
    s_g = s_b + GLA_RANK
    s_c = s_g + 2 * W_KEYS + 2 * W_BRANCH
    s_br = s_b - W_BRANCH
    s_cg = s_c - W_BRANCH
    pad = jnp.zeros(w.shape[:-1] + (N_MAIN - C_BG - GLA_RANK,), F32)
    parts = [w[..., s_c:], w[..., s_br:s_b], w[..., s_cg:s_c],
             w[..., :s_br], w[..., s_g:s_cg], w[..., s_b:s_g], pad]
    return jnp.concatenate([part.astype(BF16) for part in parts], axis=-1)


def kernel(x_prompt, x_sample, cache_k, cache_v, state_gla, state_ret, page_table, w_in, w_gla_up, b_gla, lambda_qk, rel_bias, norm_a, norm_b, norm_c, w_a, w_b, w_c, w_out, norm_ffn1, ffn1_wg, ffn1_wu, ffn1_wd, norm_mix, norm_ffn2, ffn2_wg, ffn2_wu, ffn2_wd, norm_final):
    batch, seq, _ = x_prompt.shape
    db, dec_seq, _ = x_sample.shape
    depth = w_in.shape[0]
    assert batch == 1 and dec_seq == 1
    n_pages = page_table.shape[1]
    past = n_pages * PAGE_SIZE
    n_pool = cache_k.shape[1]
    n_pages_step = math.gcd(DECODE_PAGES_PER_STEP, n_pages)

    band, dec_bias = _bias_tables(rel_bias, past, n_pages_step * PAGE_SIZE)
    cos_p, sin_p = _rope_tables(jnp.arange(seq))
    cos_s, sin_s = _rope_tables(past + jnp.arange(dec_seq))
    ck4 = cache_k.reshape(depth, n_pool, PAGE_ROWS, D_HEAD)
    cv4 = cache_v.reshape(depth, n_pool, PAGE_ROWS, D_HEAD)
    g_final = norm_final.reshape(1, D_MODEL)

    xp = x_prompt.reshape(seq, D_MODEL)
    xs = x_sample.reshape(db, D_MODEL)
    w_in_b = _reorder_w_in(w_in)
    ffn1_w = tuple(w.astype(BF16) for w in (ffn1_wg, ffn1_wu, ffn1_wd))
    ffn2_w = tuple(w.astype(BF16) for w in (ffn2_wg, ffn2_wu, ffn2_wd))
    proj_w = tuple(w.astype(BF16) for w in (w_a, w_b, w_c, w_out))
    outs = {k: [] for k in ("kp", "vp", "gp", "rp", "ks", "vs", "gs", "rs")}
    for l in range(depth):
        lam_init = 0.8 - 0.6 * math.exp(-0.3 * l)
        last = l == depth - 1
        wup = jnp.concatenate([w_gla_up[l], jnp.zeros((LANES - GLA_RANK, W_KEYS), F32)], axis=0)
        bgla = b_gla[l].reshape(1, W_KEYS)
        nb = norm_b[l].reshape(1, W_BRANCH)
        nc = norm_c[l].reshape(1, W_BRANCH)
        na = norm_a[l].reshape(1, D_HEAD)
        na_col = norm_a[l].reshape(D_HEAD, 1)
        ffn1 = (norm_ffn1[l].reshape(1, D_MODEL), *ffn1_w, l)
        ffn2 = (norm_ffn2[l].reshape(1, D_MODEL), *ffn2_w, l)
        g_mix = norm_mix[l].reshape(1, D_MODEL)

        xp = _ffn_half(xp, *ffn1, g_final, False)
        p, gate_cols, qkv, k_new, v_new = _inproj(xp, g_mix, w_in_b, l)
        aot = _prompt_attention(qkv, band, lambda_qk[l], na_col, lam_init)
        bo, co, sg, sr = _prompt_linear_attention(p, cos_p, sin_p, wup, bgla, nb, nc)
        xp = _merge(xp, aot, bo, co, gate_cols, *proj_w, l)
        xp = _ffn_half(xp, *ffn2, g_final, last)
        outs["kp"].append(k_new[None])
        outs["vp"].append(v_new[None])
        outs["gp"].append(sg[None])
        outs["rp"].append(sr[None])

        xs = _ffn_half(xs, *ffn1, g_final, False)
        p_s, gate_cols, _, k_new, v_new = _inproj(xs, g_mix, w_in_b, l)
        ao = _sample_attention(p_s, ck4, cv4, l, page_table, dec_bias, rel_bias, lambda_qk[l], na,
                               lam_init, n_pages_step)
        bo, co, sg, sr = _sample_linear_attention(p_s, cos_s, sin_s, wup, bgla, nb, nc,
                                                  state_gla, state_ret, l)
        xs = _merge(xs, ao.T, bo, co, gate_cols, *proj_w, l)
        xs = _ffn_half(xs, *ffn2, g_final, last)
        outs["ks"].append(k_new[:, None])
        outs["vs"].append(v_new[:, None])
        outs["gs"].append(sg)
        outs["rs"].append(sr)

    st = lambda key: jnp.stack(outs[key])
    return (xp.reshape(batch, seq, D_MODEL), xs.reshape(db, dec_seq, D_MODEL),
            st("kp"), st("vp"), st("gp"), st("rp"), st("ks"), st("vs"), st("gs"), st("rs"))
```

```python
import functools
import math

import jax
import jax.numpy as jnp
from jax import lax
from jax.experimental import pallas as pl
from jax.experimental.pallas import tpu as pltpu

F32 = jnp.float32
BF16 = jnp.bfloat16

D_MODEL = 1024
N_HEADS = 4
DH_A = 64
D_HEAD = 128
DK = 64
GLA_RANK = 16
GLA_TAU = 16.0
ROPE_BASE = 10000.0
NUM_BUCKETS = 32
MAX_DISTANCE = 128
D_FF = 2816
PAGE_SIZE = 128
CHUNK = 64
EPS = 1e-6
NEG = -1e30
LOG2E = math.log2(math.e)
W_BRANCH = N_HEADS * D_HEAD
W_KEYS = N_HEADS * DK
PAGE_ROWS = PAGE_SIZE * N_HEADS

LANES = 128
VMEM_LIMIT_BYTES = 56 * 1024 * 1024

C_GATES = 0
C_BR = 3072
C_CG = 3584
N_GATE = 4096
C_AQ = 0
C_AK = 512
C_AV = 1024
C_BQ = 1536
C_BK = 1792
C_BV = 2048
C_CQ = 2560
C_CK = 2816
C_CV = 3072
C_BG = 3584
N_MAIN = 3712
N_PROJ = N_GATE + N_MAIN

FAR_DISTANCE = 113


def _params(n_axes=1, vmem=VMEM_LIMIT_BYTES):
    return pltpu.CompilerParams(dimension_semantics=("arbitrary",) * n_axes,
                                vmem_limit_bytes=vmem)


def _resident(shape, index_map):
    return pl.BlockSpec(shape, index_map, pipeline_mode=pl.Buffered(1))


def _rms(x, g):
    return x * lax.rsqrt(jnp.mean(x * x, axis=-1, keepdims=True) + EPS) * g


def _sigmoid(x):
    return 1.0 / (1.0 + jnp.exp(-x))


def _dot(a, b):
    return jnp.dot(a, b, preferred_element_type=F32)


def _dot_nt(a, b):
    return lax.dot_general(a, b, (((1,), (1,)), ((), ())), preferred_element_type=F32)


def _dot_tn(a, b):
    return lax.dot_general(a, b, (((0,), (0,)), ((), ())), preferred_element_type=F32)


FF_CHUNK = 1408


def _ffn_kernel(x_ref, g_ref, wg_ref, wu_ref, wd_ref, gf_ref, o_ref, *, final_norm):
    x = x_ref[...]
    n = _rms(x, g_ref[...]).astype(BF16)
    acc = None
    for c in range(0, D_FF, FF_CHUNK):
        a = _dot(n, wg_ref[:, c:c + FF_CHUNK])
        u = _dot(n, wu_ref[:, c:c + FF_CHUNK])
        h = (a * _sigmoid(a) * u).astype(BF16)
        part = _dot(h, wd_ref[c:c + FF_CHUNK, :])
        acc = part if acc is None else acc + part
    y = x + 0.5 * acc
    if final_norm:
        y = _rms(y, gf_ref[...])
    o_ref[...] = y


def _ffn_half(x, g, wg, wu, wd, layer, g_final, final_norm):
    m = x.shape[0]
    tm = min(512, m)
    row = lambda i: (i, 0)
    const = lambda i: (0, 0)
    of_layer = lambda i: (layer, 0, 0)
    return pl.pallas_call(
        functools.partial(_ffn_kernel, final_norm=final_norm),
        grid=(m // tm,),
        in_specs=[pl.BlockSpec((tm, D_MODEL), row),
                  _resident((1, D_MODEL), const),
                  _resident((None, D_MODEL, D_FF), of_layer),
                  _resident((None, D_MODEL, D_FF), of_layer),
                  _resident((None, D_FF, D_MODEL), of_layer),
                  _resident((1, D_MODEL), const)],
        out_specs=pl.BlockSpec((tm, D_MODEL), row),
        out_shape=jax.ShapeDtypeStruct((m, D_MODEL), F32),
        compiler_params=_params(),
        name="ffn_half",
    )(x, g, wg, wu, wd, g_final)


PROJ_CHUNK = 1024


def _inproj_kernel(x_ref, g_ref, w_ref, p_ref, gate_ref, qkv_ref, k_ref, v_ref):
    n = _rms(x_ref[...], g_ref[...]).astype(BF16)
    for c in range(0, N_GATE, PROJ_CHUNK):
        gate_ref[:, c:c + PROJ_CHUNK] = _dot(n, w_ref[:, c:c + PROJ_CHUNK]).astype(BF16)
    for c in range(0, N_MAIN, PROJ_CHUNK):
        w = min(PROJ_CHUNK, N_MAIN - c)
        p_ref[:, c:c + w] = _dot(n, w_ref[:, N_GATE + c:N_GATE + c + w])
    qkv_ref[:, 0:W_BRANCH] = (p_ref[:, C_AQ:C_AQ + W_BRANCH] * (DH_A ** -0.5 * LOG2E)).astype(BF16)
    qkv_ref[:, W_BRANCH:3 * W_BRANCH] = p_ref[:, C_AK:C_AK + 2 * W_BRANCH].astype(BF16)
    for h in range(N_HEADS):
        k_ref[:, h, :] = p_ref[:, C_AK + h * D_HEAD:C_AK + (h + 1) * D_HEAD]
        v_ref[:, h, :] = p_ref[:, C_AV + h * D_HEAD:C_AV + (h + 1) * D_HEAD]


def _inproj(x, g, w, layer):
    m = x.shape[0]
    tm = min(256, m)
    row = lambda i: (i, 0)
    const = lambda i: (0, 0)
    rows3 = pl.BlockSpec((tm, N_HEADS, D_HEAD), lambda i: (i, 0, 0))
    return pl.pallas_call(
        _inproj_kernel,
        grid=(m // tm,),
        in_specs=[pl.BlockSpec((tm, D_MODEL), row),
                  _resident((1, D_MODEL), const),
                  _resident((None, D_MODEL, N_PROJ), lambda i: (layer, 0, 0))],
        out_specs=[pl.BlockSpec((tm, N_MAIN), row),
                   pl.BlockSpec((tm, N_GATE), row),
                   pl.BlockSpec((tm, 3 * W_BRANCH), row),
                   rows3, rows3],
        out_shape=[jax.ShapeDtypeStruct((m, N_MAIN), F32),
                   jax.ShapeDtypeStruct((m, N_GATE), BF16),
                   jax.ShapeDtypeStruct((m, 3 * W_BRANCH), BF16),
                   jax.ShapeDtypeStruct((m, N_HEADS, D_HEAD), F32),
                   jax.ShapeDtypeStruct((m, N_HEADS, D_HEAD), F32)],
        compiler_params=_params(),
        name="inproj",
    )(x, g, w)


def _bucket_of(rel):
    n = jnp.maximum(rel, 0)
    max_exact = NUM_BUCKETS // 2
    nf = jnp.maximum(n, max_exact).astype(F32)
    large = max_exact + (jnp.log(nf / max_exact) / math.log(MAX_DISTANCE / max_exact)
                         * (NUM_BUCKETS - max_exact)).astype(jnp.int32)
    large = jnp.minimum(large, NUM_BUCKETS - 1)
    return jnp.where(n < max_exact, n, large)


def _bias_of(bucket, rb_ref, head):
    val = jnp.zeros(bucket.shape, F32)
    for b in range(NUM_BUCKETS):
        val = jnp.where(bucket == b, rb_ref[b, head], val)
    return val


def _bias_kernel(rb_ref, band_ref, dec_ref, *, past, group_width):
    kk = lax.broadcasted_iota(jnp.int32, (LANES, LANES), 0)
    qq = lax.broadcasted_iota(jnp.int32, (LANES, LANES), 1)
    for d in range(2):
        rel = qq - kk + d * LANES
        bucket = _bucket_of(rel)
        for h in range(N_HEADS):
            val = (_bias_of(bucket, rb_ref, h) - rb_ref[NUM_BUCKETS - 1, h]) * LOG2E
            band_ref[d, h] = jnp.where(rel >= 0, val, NEG)
    rows, width = dec_ref.shape
    row = lax.broadcasted_iota(jnp.int32, (rows, width), 0)
    col = lax.broadcasted_iota(jnp.int32, (rows, width), 1)
    rel = past - ((row >> 3) * group_width + (col >> 2))
    bucket = _bucket_of(rel)
    head = col & (N_HEADS - 1)
    own = head == (row & (N_HEADS - 1))
    val = jnp.zeros((rows, width), F32)
    for h in range(N_HEADS):
        val = jnp.where(head == h, _bias_of(bucket, rb_ref, h), val)
    dec_ref[...] = jnp.where(own, val, NEG)


def _bias_tables(rel_bias, past, group_width):
    n_groups = past // group_width
    return pl.pallas_call(
        functools.partial(_bias_kernel, past=past, group_width=group_width),
        in_specs=[pl.BlockSpec(memory_space=pltpu.SMEM)],
        out_shape=[jax.ShapeDtypeStruct((2, N_HEADS, LANES, LANES), F32),
                   jax.ShapeDtypeStruct((n_groups * 8, group_width * N_HEADS), F32)],
        compiler_params=pltpu.CompilerParams(vmem_limit_bytes=VMEM_LIMIT_BYTES),
        name="bias_tables",
    )(rel_bias)


def _lambda(lq_ref, lam_init):
    lq = lq_ref[...]
    a = jnp.sum(lq[0:1] * lq[1:2], axis=-1, keepdims=True)
    b = jnp.sum(lq[2:3] * lq[3:4], axis=-1, keepdims=True)
    return jnp.exp(a) - jnp.exp(b) + lam_init


def _first_map_lanes(rows):
    lane = lax.broadcasted_iota(jnp.int32, (rows, D_HEAD), 1)
    return lane < DH_A


SOFTMAX_STRIP = 2 * LANES


def _attn_kernel(q_ref, k_ref, v_ref, band_ref, lq_ref, na_ref, o_ref,
                 qbd_sc, m_sc, l_sc, acc_sc, s_sc, p_sc, a_sc, sn_sc, pn_sc, *, t, lam_init):
    i = pl.program_id(0)
    nsub = t // LANES
    first_map = _first_map_lanes(t)

    for h in range(N_HEADS):
        q_h = q_ref[:, h * D_HEAD:(h + 1) * D_HEAD]
        zero = jnp.zeros_like(q_h)
        qbd_sc[h, 0:t, :] = jnp.where(first_map, q_h, zero)
        qbd_sc[h, t:2 * t, :] = jnp.where(first_map, zero, q_h)
    m_sc[...] = jnp.full(m_sc.shape, NEG, F32)
    l_sc[...] = jnp.zeros(l_sc.shape, F32)
    acc_sc[...] = jnp.zeros(acc_sc.shape, F32)

    def stage_scores(h, ks, slot, lead_of, s_buf):
        s = _dot_nt(k_ref[pl.ds(ks, t), h * D_HEAD:(h + 1) * D_HEAD], qbd_sc[h])
        if lead_of is not None:
            rows = []
            for kc in range(nsub):
                cols = []
                for mp in range(2):
                    for qr in range(nsub):
                        sub = s[kc * LANES:(kc + 1) * LANES, mp * t + qr * LANES:mp * t + (qr + 1) * LANES]
                        lead = lead_of(kc, qr)
                        if lead < 0:
                            sub = jnp.full_like(sub, NEG)
                        elif lead < 2:
                            sub = sub + band_ref[lead, h]
                        cols.append(sub)
                rows.append(jnp.concatenate(cols, axis=1))
            s = jnp.concatenate(rows, axis=0) if nsub > 1 else rows[0]
        s_buf[slot] = s

    def stage_softmax(h, slot, s_buf, p_buf):
        m_prev = m_sc[h]
        m_new = jnp.maximum(m_prev, jnp.max(s_buf[slot], axis=0, keepdims=True))
        alpha = jnp.exp2(m_prev - m_new)
        sums = []
        for c0 in range(0, 2 * t, SOFTMAX_STRIP):
            cs = slice(c0, c0 + SOFTMAX_STRIP)
            p = jnp.exp2(s_buf[slot, :, cs] - m_new[:, cs])
            sums.append(jnp.sum(p, axis=0, keepdims=True))
            p_buf[slot, :, cs] = p.astype(BF16)
        l_sc[h] = alpha * l_sc[h] + jnp.concatenate(sums, axis=1)
        a_sc[h] = alpha
        m_sc[h] = m_new

    def stage_values(h, ks, slot, p_buf):
        v_h = v_ref[pl.ds(ks, t), h * D_HEAD:(h + 1) * D_HEAD]
        acc_sc[h] = a_sc[h] * acc_sc[h] + _dot_tn(v_h, p_buf[slot])

    def kv_block(ks, lead_of, s_buf, p_buf):
        stage_scores(0, ks, 0, lead_of, s_buf)
        for h in range(N_HEADS):
            if h + 1 < N_HEADS:
                stage_scores(h + 1, ks, (h + 1) % 2, lead_of, s_buf)
            stage_softmax(h, h % 2, s_buf, p_buf)
            stage_values(h, ks, h % 2, p_buf)

    def far_pair(j, carry):
        ks0 = pl.multiple_of(j * (2 * t), 2 * t)
        steps = [(h, pl.multiple_of(ks0 + blk * t, t)) for blk in range(2) for h in range(N_HEADS)]
        stage_scores(steps[0][0], steps[0][1], 0, None, s_sc)
        for n, (h, ks) in enumerate(steps):
            if n + 1 < len(steps):
                stage_scores(steps[n + 1][0], steps[n + 1][1], (n + 1) % 2, None, s_sc)
            stage_softmax(h, n % 2, s_sc, p_sc)
            stage_values(h, ks, n % 2, p_sc)
        return carry

    n_far = jnp.maximum(i - 1, 0)
    lax.fori_loop(0, n_far // 2, far_pair, 0)

    @pl.when(n_far % 2 == 1)
    def _():
        kv_block(pl.multiple_of((n_far - 1) * t, t), None, sn_sc, pn_sc)

    @pl.when(i >= 1)
    def _():
        kv_block(pl.multiple_of((i - 1) * t, t), lambda kc, qr: qr - kc + nsub, sn_sc, pn_sc)

    kv_block(pl.multiple_of(i * t, t), lambda kc, qr: qr - kc, sn_sc, pn_sc)
    lam = _lambda(lq_ref, lam_init)
    for h in range(N_HEADS):
        o = acc_sc[h] / l_sc[h]
        attn = o[:, 0:t] - lam * o[:, t:2 * t]
        norm = lax.rsqrt(jnp.mean(attn * attn, axis=0, keepdims=True) + EPS)
        o_ref[h * D_HEAD:(h + 1) * D_HEAD, :] = (
            attn * norm * na_ref[...] * (1.0 - lam_init)).astype(o_ref.dtype)


def _prompt_attention(qkv, band, lambda_qk, norm_a_col, lam_init):
    seq = qkv.shape[0]
    t = min(512, seq)
    assert t % LANES == 0 and seq % t == 0 and t + 1 >= FAR_DISTANCE
    const2 = lambda i: (0, 0)
    return pl.pallas_call(
        functools.partial(_attn_kernel, t=t, lam_init=lam_init),
        grid=(seq // t,),
        in_specs=[pl.BlockSpec((t, W_BRANCH), lambda i: (i, 0)),
                  _resident((seq, W_BRANCH), lambda i: (0, 1)),
                  _resident((seq, W_BRANCH), lambda i: (0, 2)),
                  _resident((2, N_HEADS, LANES, LANES), lambda i: (0, 0, 0, 0)),
                  _resident((4, DH_A), const2),
                  _resident((D_HEAD, 1), const2)],
        out_specs=pl.BlockSpec((W_BRANCH, t), lambda i: (0, i)),
        out_shape=jax.ShapeDtypeStruct((W_BRANCH, seq), BF16),
        scratch_shapes=[pltpu.VMEM((N_HEADS, 2 * t, D_HEAD), BF16),
                        pltpu.VMEM((N_HEADS, 1, 2 * t), F32),
                        pltpu.VMEM((N_HEADS, 1, 2 * t), F32),
                        pltpu.VMEM((N_HEADS, D_HEAD, 2 * t), F32),
                        pltpu.VMEM((2, t, 2 * t), F32),
                        pltpu.VMEM((2, t, 2 * t), BF16),
                        pltpu.VMEM((N_HEADS, 1, 2 * t), F32),
                        pltpu.VMEM((2, t, 2 * t), F32),
                        pltpu.VMEM((2, t, 2 * t), BF16)],
        compiler_params=_params(),
        name="prompt_attention",
    )(qkv, qkv, qkv, band, lambda_qk, norm_a_col)


DECODE_PAGES_PER_STEP = 16
DECODE_LOOKAHEAD = 2
DECODE_SLOTS = DECODE_LOOKAHEAD + 1


def _rows_by_head(x_row):
    pieces = [x_row[:, h * D_HEAD:(h + 1) * D_HEAD] for h in range(N_HEADS)]
    return jnp.concatenate(pieces + pieces, axis=0)


def _decode_kernel(pt_ref, ck_hbm, cv_hbm, dec_ref, rb_ref, lq_ref, na_ref, q_ref, ks_ref, vs_ref, o_ref,
                   kbuf, vbuf, sems, qrows_sc, m_sc, l_sc, acc_sc, *, layer, n_pages_step, lam_init):
    b = pl.program_id(0)
    g = pl.program_id(1)
    n_b = pl.num_programs(0)
    n_groups = pl.num_programs(1)
    step = b * n_groups + g
    n_steps = n_b * n_groups
    slot = lax.rem(step, DECODE_SLOTS)

    def page_copies(n):
        bb = lax.div(n, n_groups)
        gg = lax.rem(n, n_groups)
        sl = lax.rem(n, DECODE_SLOTS)
        copies = []
        for r in range(n_pages_step):
            page = pt_ref[bb, gg * n_pages_step + r]
            copies.append(pltpu.make_async_copy(ck_hbm.at[layer, page], kbuf.at[sl, r], sems.at[sl, 0]))
            copies.append(pltpu.make_async_copy(cv_hbm.at[layer, page], vbuf.at[sl, r], sems.at[sl, 1]))
        return copies

    for ahead in range(DECODE_LOOKAHEAD):
        @pl.when((step == 0) & (ahead < n_steps))
        def _():
            for cp in page_copies(step + ahead):
                cp.start()

    @pl.when(step + DECODE_LOOKAHEAD < n_steps)
    def _():
        for cp in page_copies(step + DECODE_LOOKAHEAD):
            cp.start()

    n_rows = 2 * N_HEADS
    row = lax.broadcasted_iota(jnp.int32, (n_rows, D_HEAD), 0)
    lane = lax.broadcasted_iota(jnp.int32, (n_rows, D_HEAD), 1)
    own_map = (lane >> 6) == (row >> 2)

    @pl.when(g == 0)
    def _():
        q8 = _rows_by_head(q_ref[...] * (DH_A ** -0.5))
        qrows_sc[...] = jnp.where(own_map, q8, 0.0)
        m_sc[...] = jnp.full(m_sc.shape, NEG, F32)
        l_sc[...] = jnp.zeros(l_sc.shape, F32)
        acc_sc[...] = jnp.zeros(acc_sc.shape, F32)

    qrows = qrows_sc[...]
    qb = qrows.astype(BF16)

    for cp in page_copies(step):
        cp.wait()
    k_pages = [kbuf[slot, r] for r in range(n_pages_step)]
    v_pages = [vbuf[slot, r] for r in range(n_pages_step)]

    s = jnp.concatenate([_dot_nt(qb, k_pages[r].astype(BF16)) for r in range(n_pages_step)], axis=1)
    s = s + dec_ref[...]
    m_prev = m_sc[...]
    m_new = jnp.maximum(m_prev, jnp.max(s, axis=-1, keepdims=True))
    alpha = jnp.exp(m_prev - m_new)
    p = jnp.exp(s - m_new)
    l_sc[...] = alpha * l_sc[...] + jnp.sum(p, axis=-1, keepdims=True)
    pb = p.astype(BF16)
    pv = None
    for r in range(n_pages_step):
        part = _dot(pb[:, r * PAGE_ROWS:(r + 1) * PAGE_ROWS], v_pages[r].astype(BF16))
        pv = part if pv is None else pv + part
    acc_sc[...] = alpha * acc_sc[...] + pv
    m_sc[...] = m_new

    @pl.when(g == pl.num_programs(1) - 1)
    def _():
        head = lax.broadcasted_iota(jnp.int32, (n_rows, 1), 0) & (N_HEADS - 1)
        b0 = jnp.zeros((n_rows, 1), F32)
        for h in range(N_HEADS):
            b0 = jnp.where(head == h, rb_ref[0, h], b0)
        s_own = jnp.sum(qrows * _rows_by_head(ks_ref[...]), axis=-1, keepdims=True) + b0
        m_prev = m_sc[...]
        m_fin = jnp.maximum(m_prev, s_own)
        alpha = jnp.exp(m_prev - m_fin)
        p_own = jnp.exp(s_own - m_fin)
        l_fin = alpha * l_sc[...] + p_own
        acc = alpha * acc_sc[...] + p_own * _rows_by_head(vs_ref[...])
        o = acc / l_fin
        lam = _lambda(lq_ref, lam_init)
        attn = o[0:N_HEADS] - lam * o[N_HEADS:n_rows]
        out = _rms(attn, na_ref[...]) * (1.0 - lam_init)
        for h in range(N_HEADS):
            o_ref[:, h * D_HEAD:(h + 1) * D_HEAD] = out[h:h + 1].astype(o_ref.dtype)


def _sample_attention(p_s, cache_k, cache_v, layer, page_table, dec_bias, rel_bias, lambda_qk, norm_a,
                      lam_init, n_pages_step):
    db = p_s.shape[0]
    n_pages = page_table.shape[1]
    n_groups = n_pages // n_pages_step
    gw = n_pages_step * PAGE_ROWS
    p3 = p_s.reshape(db, 1, N_MAIN)
    dec3 = dec_bias.reshape(n_groups, 8, gw)

    def row_spec(col):
        return pl.BlockSpec((None, 1, W_BRANCH), lambda b, g, pt: (b, 0, col // W_BRANCH))

    const2 = lambda b, g, pt: (0, 0)
    page_buffers = pltpu.VMEM((DECODE_SLOTS, n_pages_step, PAGE_ROWS, D_HEAD), F32)
    grid_spec = pltpu.PrefetchScalarGridSpec(
        num_scalar_prefetch=1,
        grid=(db, n_groups),
        in_specs=[pl.BlockSpec(memory_space=pl.ANY),
                  pl.BlockSpec(memory_space=pl.ANY),
                  pl.BlockSpec((None, 8, gw), lambda b, g, pt: (g, 0, 0)),
                  pl.BlockSpec(memory_space=pltpu.SMEM),
                  pl.BlockSpec((4, DH_A), const2),
                  pl.BlockSpec((1, D_HEAD), const2),
                  row_spec(C_AQ), row_spec(C_AK), row_spec(C_AV)],
        out_specs=pl.BlockSpec((None, 1, W_BRANCH), lambda b, g, pt: (b, 0, 0)),
        scratch_shapes=[page_buffers, page_buffers,
                        pltpu.SemaphoreType.DMA((DECODE_SLOTS, 2)),
                        pltpu.VMEM((8, D_HEAD), F32),
                        pltpu.VMEM((8, 1), F32),
                        pltpu.VMEM((8, 1), F32),
                        pltpu.VMEM((8, D_HEAD), F32)])
    out = pl.pallas_call(
        functools.partial(_decode_kernel, layer=layer, n_pages_step=n_pages_step, lam_init=lam_init),
        grid_spec=grid_spec,
        out_shape=jax.ShapeDtypeStruct((db, 1, W_BRANCH), BF16),
        compiler_params=_params(2),
        name="sample_attention",
    )(page_table, cache_k, cache_v, dec3, rel_bias, lambda_qk, norm_a, p3, p3, p3)
    return out.reshape(db, W_BRANCH)


def _log_sigmoid(x):
    return jnp.minimum(x, 0.0) - jnp.log1p(jnp.exp(-jnp.abs(x)))


def _split_bf16(x):
    hi = x.astype(BF16)
    lo = (x - hi.astype(F32)).astype(BF16)
    return hi, lo


def _dot_3pass(a, b):
    a_hi, a_lo = _split_bf16(a)
    b_hi, b_lo = _split_bf16(b)
    return _dot(a_hi, b_hi) + (_dot(a_hi, b_lo) + _dot(a_lo, b_hi))


def _gla_log_decay(bg, wup_ref, bgla_ref):
    x = _dot_3pass(bg, wup_ref[...])
    return _log_sigmoid(x + bgla_ref[...]) / GLA_TAU


def _rotate_half(x):
    width = x.shape[-1]
    lane = lax.broadcasted_iota(jnp.int32, x.shape, x.ndim - 1)
    lo = (lane & (DK - 1)) < DK // 2
    return jnp.where(lo, pltpu.roll(x, width - DK // 2, x.ndim - 1), pltpu.roll(x, DK // 2, x.ndim - 1))


def _rotary(x, cos, sin_signed):
    reps = x.shape[-1] // cos.shape[-1]
    cos = jnp.concatenate([cos] * reps, axis=-1)
    sin_signed = jnp.concatenate([sin_signed] * reps, axis=-1)
    return x * cos + _rotate_half(x) * sin_signed


def _log_gamma(h):
    return math.log(1.0 - 2.0 ** (-5.0 - h))


def _head_of_keys(shape, dim):
    return lax.broadcasted_iota(jnp.int32, shape, dim) >> 6


def _layernorm(x, g):
    xc = x - jnp.mean(x, axis=-1, keepdims=True)
    return xc * lax.rsqrt(jnp.mean(xc * xc, axis=-1, keepdims=True) + EPS) * g


def _block_diag_mask():
    r = lax.broadcasted_iota(jnp.int32, (W_KEYS, W_BRANCH), 0) >> 6
    c = lax.broadcasted_iota(jnp.int32, (W_KEYS, W_BRANCH), 1) >> 7
    return r == c


def _head_blocks(full):
    return jnp.concatenate([full[h * DK:(h + 1) * DK, h * D_HEAD:(h + 1) * D_HEAD]
                            for h in range(N_HEADS)], axis=0)


GROUP = 256


def _linattn_kernel(bq_ref, bk_ref, bv_ref, bg_ref, cq_ref, ck_ref, cv_ref, cos_ref, sin_ref,
                    wup_ref, bgla_ref, nb_ref, nc_ref,
                    bo_ref, co_ref, sg_ref, sr_ref,
                    stg_sc, str_sc, *, rows):
    step = pl.program_id(0)

    @pl.when(step == 0)
    def _():
        stg_sc[...] = jnp.zeros(stg_sc.shape, F32)
        str_sc[...] = jnp.zeros(str_sc.shape, F32)

    n_chunks = rows // CHUNK
    n_groups = rows // GROUP
    cpg = GROUP // CHUNK
    ti = lax.broadcasted_iota(jnp.int32, (GROUP, GROUP), 0)
    si = lax.broadcasted_iota(jnp.int32, (GROUP, GROUP), 1)
    causal = ((ti >> 6) == (si >> 6)) & (ti >= si)
    tril = jnp.where(causal, 1.0, 0.0).astype(BF16)
    dist = (ti - si).astype(F32)
    key_head = _head_of_keys((GROUP, W_KEYS), 1)

    tcol = (lax.broadcasted_iota(jnp.int32, (GROUP, W_KEYS), 0) & (CHUNK - 1)).astype(F32)
    lg_lane = jnp.zeros((GROUP, W_KEYS), F32)
    lg_row = jnp.zeros((W_KEYS, D_HEAD), F32)
    row_head = _head_of_keys((W_KEYS, D_HEAD), 0)
    for h in range(N_HEADS):
        lg_lane = jnp.where(key_head == h, _log_gamma(h), lg_lane)
        lg_row = jnp.where(row_head == h, _log_gamma(h), lg_row)
    g_fwd = jnp.exp((tcol + 1.0) * lg_lane)
    g_rev = jnp.exp((CHUNK - 1.0 - tcol) * lg_lane)
    g_end = jnp.exp(CHUNK * lg_row)

    glog = _gla_log_decay(bg_ref[...], wup_ref, bgla_ref)
    cq_all = _rotary(cq_ref[...], cos_ref[...], sin_ref[...])
    ck_all = _rotary(ck_ref[...], cos_ref[...], sin_ref[...]) * (DK ** -0.5)
    bq_all = bq_ref[...] * (DK ** -0.5)

    g_qd, g_oi, g_upd, g_end_rows = [], [], [], []
    r_qf, r_oi, r_upd = [], [], []
    for g in range(n_groups):
        rs = slice(g * GROUP, (g + 1) * GROUP)
        g_hi, g_lo = _split_bf16(glog[rs])
        gcum = _dot(tril, g_hi) + _dot(tril, g_lo)
        ends = [gcum[(c + 1) * CHUNK - 1:(c + 1) * CHUNK] for c in range(cpg)]
        g_end_rows += ends
        gend_b = jnp.concatenate([jnp.broadcast_to(e, (CHUNK, W_KEYS)) for e in ends], axis=0)
        qd = bq_all[rs] * jnp.exp(gcum)
        bk = bk_ref[rs, :]
        kinv = (bk * jnp.exp(-gcum)).astype(BF16)
        krem = (bk * jnp.exp(gend_b - gcum)).astype(BF16)
        v = bv_ref[rs, :].astype(BF16)
        parts = []
        for h in range(N_HEADS):
            qm = jnp.where(key_head == h, qd, 0.0).astype(BF16)
            sc = jnp.where(causal, _dot_nt(qm, kinv), 0.0)
            parts.append(_dot(sc.astype(BF16), v[:, h * D_HEAD:(h + 1) * D_HEAD]))
        oi = jnp.concatenate(parts, axis=1)
        qdb = qd.astype(BF16)
        for c in range(cpg):
            cs = slice(c * CHUNK, (c + 1) * CHUNK)
            g_qd.append(qdb[cs])
            g_oi.append(oi[cs])
            g_upd.append(_head_blocks(_dot_tn(krem[cs], v[cs])))
        q = cq_all[rs]
        k = ck_all[rs]
        v = cv_ref[rs, :].astype(BF16)
        kb = k.astype(BF16)
        parts = []
        for h in range(N_HEADS):
            qm = jnp.where(key_head == h, q, 0.0).astype(BF16)
            decay = jnp.where(causal, jnp.exp(dist * _log_gamma(h)), 0.0)
            sc = _dot_nt(qm, kb) * decay
            parts.append(_dot(sc.astype(BF16), v[:, h * D_HEAD:(h + 1) * D_HEAD]))
        oi = jnp.concatenate(parts, axis=1)
        krev = (k * g_rev).astype(BF16)
        qfb = (q * g_fwd).astype(BF16)
        for c in range(cpg):
            cs = slice(c * CHUNK, (c + 1) * CHUNK)
            r_qf.append(qfb[cs])
            r_oi.append(oi[cs])
            r_upd.append(_head_blocks(_dot_tn(krev[cs], v[cs])))

    e_cols = jnp.exp(jnp.transpose(jnp.concatenate(g_end_rows, axis=0)))

    bd = _block_diag_mask()

    def state_products(qs, st):
        stb = st.astype(BF16)
        st_bd = jnp.where(bd, jnp.concatenate([stb] * N_HEADS, axis=1), jnp.zeros((), BF16))
        return _dot(qs, st_bd)

    st = stg_sc[...]
    for c in range(n_chunks):
        rs = slice(c * CHUNK, (c + 1) * CHUNK)
        o = g_oi[c] + state_products(g_qd[c], st)
        st = e_cols[:, c:c + 1] * st + g_upd[c]
        for h in range(N_HEADS):
            hs = slice(h * D_HEAD, (h + 1) * D_HEAD)
            bo_ref[rs, hs] = _rms(o[:, hs], nb_ref[:, hs]).astype(bo_ref.dtype)
    stg_sc[...] = st
    st = str_sc[...]
    for c in range(n_chunks):
        rs = slice(c * CHUNK, (c + 1) * CHUNK)
        o = r_oi[c] + state_products(r_qf[c], st)
        st = g_end * st + r_upd[c]
        for h in range(N_HEADS):
            hs = slice(h * D_HEAD, (h + 1) * D_HEAD)
            co_ref[rs, hs] = _layernorm(o[:, hs], nc_ref[:, hs]).astype(co_ref.dtype)
    str_sc[...] = st

    @pl.when(step == pl.num_programs(0) - 1)
    def _():
        for h in range(N_HEADS):
            sg_ref[h] = stg_sc[h * DK:(h + 1) * DK, :]
            sr_ref[h] = str_sc[h * DK:(h + 1) * DK, :]


def _prompt_linear_attention(p, cos, sin, wup, bgla, norm_b, norm_c):
    seq = p.shape[0]
    rows = min(512, seq)

    def col(c, w):
        return pl.BlockSpec((rows, w), lambda i: (i, c // w))

    const = lambda i: (0, 0)
    state = jax.ShapeDtypeStruct((N_HEADS, DK, D_HEAD), F32)
    state_spec = pl.BlockSpec((N_HEADS, DK, D_HEAD), lambda i: (0, 0, 0))
    return pl.pallas_call(
        functools.partial(_linattn_kernel, rows=rows),
        grid=(seq // rows,),
        in_specs=[col(C_BQ, W_KEYS), col(C_BK, W_KEYS), col(C_BV, W_BRANCH), col(C_BG, LANES),
                  col(C_CQ, W_KEYS), col(C_CK, W_KEYS), col(C_CV, W_BRANCH),
                  pl.BlockSpec((rows, LANES), lambda i: (i, 0)),
                  pl.BlockSpec((rows, LANES), lambda i: (i, 0)),
                  pl.BlockSpec((LANES, W_KEYS), const),
                  pl.BlockSpec((1, W_KEYS), const),
                  pl.BlockSpec((1, W_BRANCH), const),
                  pl.BlockSpec((1, W_BRANCH), const)],
        out_specs=[pl.BlockSpec((rows, W_BRANCH), lambda i: (i, 0)),
                   pl.BlockSpec((rows, W_BRANCH), lambda i: (i, 0)),
                   state_spec, state_spec],
        out_shape=[jax.ShapeDtypeStruct((seq, W_BRANCH), BF16),
                   jax.ShapeDtypeStruct((seq, W_BRANCH), BF16),
                   state, state],
        scratch_shapes=[pltpu.VMEM((W_KEYS, D_HEAD), F32),
                        pltpu.VMEM((W_KEYS, D_HEAD), F32)],
        compiler_params=_params(),
        name="prompt_linear_attention",
    )(p, p, p, p, p, p, p, cos, sin, wup, bgla, norm_b, norm_c)


def _to_column(x_row):
    n = x_row.shape[-1]
    r = lax.broadcasted_iota(jnp.int32, (n, n), 0)
    c = lax.broadcasted_iota(jnp.int32, (n, n), 1)
    return jnp.sum(jnp.where(r == c, jnp.broadcast_to(x_row, (n, n)), 0.0), axis=-1, keepdims=True)


def _recurrent_kernel(bq_ref, bk_ref, bv_ref, bg_ref, cq_ref, ck_ref, cv_ref, cos_ref, sin_ref,
                      wup_ref, bgla_ref, nb_ref, nc_ref, sg0_ref, sr0_ref,
                      bo_ref, co_ref, sg_ref, sr_ref):
    decay = _to_column(jnp.exp(_gla_log_decay(bg_ref[...], wup_ref, bgla_ref)))
    bq = _to_column(bq_ref[...] * (DK ** -0.5))
    bk = _to_column(bk_ref[...])
    cq = _to_column(_rotary(cq_ref[...], cos_ref[...], sin_ref[...]))
    ck = _to_column(_rotary(ck_ref[...], cos_ref[...], sin_ref[...]) * (DK ** -0.5))
    for h in range(N_HEADS):
        ks = slice(h * DK, (h + 1) * DK)
        hs = slice(h * D_HEAD, (h + 1) * D_HEAD)
        s_new = decay[ks] * sg0_ref[h] + bk[ks] * bv_ref[:, hs]
        sg_ref[h] = s_new
        o = jnp.sum(bq[ks] * s_new, axis=0, keepdims=True)
        bo_ref[:, hs] = _rms(o, nb_ref[:, hs]).astype(bo_ref.dtype)
        s_new = math.exp(_log_gamma(h)) * sr0_ref[h] + ck[ks] * cv_ref[:, hs]
        sr_ref[h] = s_new
        o = jnp.sum(cq[ks] * s_new, axis=0, keepdims=True)
        co_ref[:, hs] = _layernorm(o, nc_ref[:, hs]).astype(co_ref.dtype)


def _sample_linear_attention(p_s, cos, sin, wup, bgla, norm_b, norm_c, state_gla, state_ret, layer):
    db = p_s.shape[0]
    p3 = p_s.reshape(db, 1, N_MAIN)

    def col(c, w):
        return pl.BlockSpec((None, 1, w), lambda b: (b, 0, c // w))

    const = lambda b: (0, 0)
    state_spec = pl.BlockSpec((None, N_HEADS, DK, D_HEAD), lambda b: (b, 0, 0, 0))
    state_in_spec = pl.BlockSpec((None, None, N_HEADS, DK, D_HEAD), lambda b: (layer, b, 0, 0, 0))
    out_row = pl.BlockSpec((None, 1, W_BRANCH), lambda b: (b, 0, 0))
    state = jax.ShapeDtypeStruct((db, N_HEADS, DK, D_HEAD), F32)
    bo, co, sg, sr = pl.pallas_call(
        _recurrent_kernel,
        grid=(db,),
        in_specs=[col(C_BQ, W_KEYS), col(C_BK, W_KEYS), col(C_BV, W_BRANCH), col(C_BG, LANES),
                  col(C_CQ, W_KEYS), col(C_CK, W_KEYS), col(C_CV, W_BRANCH),
                  pl.BlockSpec((1, LANES), const), pl.BlockSpec((1, LANES), const),
                  pl.BlockSpec((LANES, W_KEYS), const),
                  pl.BlockSpec((1, W_KEYS), const),
                  pl.BlockSpec((1, W_BRANCH), const),
                  pl.BlockSpec((1, W_BRANCH), const),
                  state_in_spec, state_in_spec],
        out_specs=[out_row, out_row, state_spec, state_spec],
        out_shape=[jax.ShapeDtypeStruct((db, 1, W_BRANCH), BF16),
                   jax.ShapeDtypeStruct((db, 1, W_BRANCH), BF16),
                   state, state],
        compiler_params=_params(),
        name="sample_linear_attention",
    )(p3, p3, p3, p3, p3, p3, p3, cos, sin, wup, bgla, norm_b, norm_c, state_gla, state_ret)
    return bo.reshape(db, W_BRANCH), co.reshape(db, W_BRANCH), sg, sr


def _merge_kernel(x_ref, aot_ref, bo_ref, co_ref, br_ref, cg_ref, gates_ref,
                  wa_ref, wb_ref, wc_ref, wo_ref, o_ref):
    ya = _dot_tn(aot_ref[...], wa_ref[...])
    br = br_ref[...].astype(F32)
    yb = _dot((bo_ref[...].astype(F32) * (br * _sigmoid(br))).astype(BF16), wb_ref[...])
    cg = cg_ref[...].astype(F32)
    yc = _dot((co_ref[...].astype(F32) * (cg * _sigmoid(cg))).astype(BF16), wc_ref[...])
    mix = (_sigmoid(gates_ref[:, 0:D_MODEL].astype(F32)) * ya
           + _sigmoid(gates_ref[:, D_MODEL:2 * D_MODEL].astype(F32)) * yb
           + _sigmoid(gates_ref[:, 2 * D_MODEL:3 * D_MODEL].astype(F32)) * yc)
    o_ref[...] = x_ref[...] + _dot(mix.astype(BF16), wo_ref[...])


def _merge(x, aot, bo, co, p, wa, wb, wc, wo, layer):
    m = x.shape[0]
    tm = min(512, m)
    row = lambda i: (i, 0)
    of_layer = lambda i: (layer, 0, 0)
    return pl.pallas_call(
        _merge_kernel,
        grid=(m // tm,),
        in_specs=[pl.BlockSpec((tm, D_MODEL), row),
                  pl.BlockSpec((W_BRANCH, tm), lambda i: (0, i)),
                  pl.BlockSpec((tm, W_BRANCH), row),
                  pl.BlockSpec((tm, W_BRANCH), row),
                  pl.BlockSpec((tm, W_BRANCH), lambda i: (i, C_BR // W_BRANCH)),
                  pl.BlockSpec((tm, W_BRANCH), lambda i: (i, C_CG // W_BRANCH)),
                  pl.BlockSpec((tm, 3 * D_MODEL), lambda i: (i, C_GATES // (3 * D_MODEL))),
                  _resident((None, W_BRANCH, D_MODEL), of_layer),
                  _resident((None, W_BRANCH, D_MODEL), of_layer),
                  _resident((None, W_BRANCH, D_MODEL), of_layer),
                  _resident((None, D_MODEL, D_MODEL), of_layer)],
        out_specs=pl.BlockSpec((tm, D_MODEL), row),
        out_shape=jax.ShapeDtypeStruct((m, D_MODEL), F32),
        compiler_params=_params(),
        name="merge",
    )(x, aot, bo, co, p, p, p, wa, wb, wc, wo)


def _rope_tables(pos):
    half = DK // 2
    inv = ROPE_BASE ** (-jnp.arange(half, dtype=F32) / half)
    ang = pos.astype(F32)[:, None] * inv[None, :]
    cos = jnp.cos(ang)
    sin = jnp.sin(ang)
    cos_t = jnp.tile(jnp.concatenate([cos, cos], axis=-1), (1, LANES // DK))
    sin_t = jnp.tile(jnp.concatenate([-sin, sin], axis=-1), (1, LANES // DK))
    return cos_t, sin_t


def _reorder_w_in(w):
    s_a = 3 * W_BRANCH
    s_b = s_a + 2 * W_KEYS + 2 * W_BRANCH
    s_g = s_b + GLA_RANK
    s_c = s_g + 2 * W_KEYS + 2 * W_BRANCH
    s_br = s_b - W_BRANCH
    s_cg = s_c - W_BRANCH
    pad = jnp.zeros(w.shape[:-1] + (N_MAIN - C_BG - GLA_RANK,), F32)
    parts = [w[..., s_c:], w[..., s_br:s_b], w[..., s_cg:s_c],
             w[..., :s_br], w[..., s_g:s_cg], w[..., s_b:s_g], pad]
    return jnp.concatenate([part.astype(BF16) for part in parts], axis=-1)


def kernel(x_prompt, x_sample, cache_k, cache_v, state_gla, state_ret, page_table, w_in, w_gla_up, b_gla, lambda_qk, rel_bias, norm_a, norm_b, norm_c, w_a, w_b, w_c, w_out, norm_ffn1, ffn1_wg, ffn1_wu, ffn1_wd, norm_mix, norm_ffn2, ffn2_wg, ffn2_wu, ffn2_wd, norm_final):
    batch, seq, _ = x_prompt.shape
    db, dec_seq, _ = x_sample.shape
    depth = w_in.shape[0]
    assert batch == 1 and dec_seq == 1
    n_pages = page_table.shape[1]
    past = n_pages * PAGE_SIZE
    n_pool = cache_k.shape[1]
    n_pages_step = math.gcd(DECODE_PAGES_PER_STEP, n_pages)

    band, dec_bias = _bias_tables(rel_bias, past, n_pages_step * PAGE_SIZE)
    cos_p, sin_p = _rope_tables(jnp.arange(seq))
    cos_s, sin_s = _rope_tables(past + jnp.arange(dec_seq))
    ck4 = cache_k.reshape(depth, n_pool, PAGE_ROWS, D_HEAD)
    cv4 = cache_v.reshape(depth, n_pool, PAGE_ROWS, D_HEAD)
    g_final = norm_final.reshape(1, D_MODEL)

    xp = x_prompt.reshape(seq, D_MODEL)
    xs = x_sample.reshape(db, D_MODEL)
    w_in_b = _reorder_w_in(w_in)
    ffn1_w = tuple(w.astype(BF16) for w in (ffn1_wg, ffn1_wu, ffn1_wd))
    ffn2_w = tuple(w.astype(BF16) for w in (ffn2_wg, ffn2_wu, ffn2_wd))
    proj_w = tuple(w.astype(BF16) for w in (w_a, w_b, w_c, w_out))
    outs = {k: [] for k in ("kp", "vp", "gp", "rp", "ks", "vs", "gs", "rs")}
    for l in range(depth):
        lam_init = 0.8 - 0.6 * math.exp(-0.3 * l)
        last = l == depth - 1
        wup = jnp.concatenate([w_gla_up[l], jnp.zeros((LANES - GLA_RANK, W_KEYS), F32)], axis=0)
        bgla = b_gla[l].reshape(1, W_KEYS)
        nb = norm_b[l].reshape(1, W_BRANCH)
        nc = norm_c[l].reshape(1, W_BRANCH)
        na = norm_a[l].reshape(1, D_HEAD)
        na_col = norm_a[l].reshape(D_HEAD, 1)
        ffn1 = (norm_ffn1[l].reshape(1, D_MODEL), *ffn1_w, l)
        ffn2 = (norm_ffn2[l].reshape(1, D_MODEL), *ffn2_w, l)
        g_mix = norm_mix[l].reshape(1, D_MODEL)

        xp = _ffn_half(xp, *ffn1, g_final, False)
        p, gate_cols, qkv, k_new, v_new = _inproj(xp, g_mix, w_in_b, l)
        aot = _prompt_attention(qkv, band, lambda_qk[l], na_col, lam_init)
        bo, co, sg, sr = _prompt_linear_attention(p, cos_p, sin_p, wup, bgla, nb, nc)
        xp = _merge(xp, aot, bo, co, gate_cols, *proj_w, l)
        xp = _ffn_half(xp, *ffn2, g_final, last)
        outs["kp"].append(k_new[None])
        outs["vp"].append(v_new[None])
        outs["gp"].append(sg[None])
        outs["rp"].append(sr[None])

        xs = _ffn_half(xs, *ffn1, g_final, False)
        p_s, gate_cols, _, k_new, v_new = _inproj(xs, g_mix, w_in_b, l)
        ao = _sample_attention(p_s, ck4, cv4, l, page_table, dec_bias, rel_bias, lambda_qk[l], na,
                               lam_init, n_pages_step)
        bo, co, sg, sr = _sample_linear_attention(p_s, cos_s, sin_s, wup, bgla, nb, nc,
                                                  state_gla, state_ret, l)
        xs = _merge(xs, ao.T, bo, co, gate_cols, *proj_w, l)
        xs = _ffn_half(xs, *ffn2, g_final, last)
        outs["ks"].append(k_new[:, None])
        outs["vs"].append(v_new[:, None])
        outs["gs"].append(sg)
        outs["rs"].append(sr)

    st = lambda key: jnp.stack(outs[key])
    return (xp.reshape(batch, seq, D_MODEL), xs.reshape(db, dec_seq, D_MODEL),
            st("kp"), st("vp"), st("gp"), st("rp"), st("ks"), st("vs"), st("gs"), st("rs"))
```
